```python
import jax, jax.numpy as jnp
from jax import lax
import numpy as np

D_MODEL = 2048
BATCH = 2
SEQ = 8192
DEPTH = 2
DEC_BATCH = 32
DEC_SEQ = 16
PAST_LEN = 4096

CHUNK = 64
EPS = 1e-6
D_RNN = D_MODEL
RNN_BLOCKS = 16
RNN_BLOCK = D_RNN // RNN_BLOCKS
CONV_W = 4
LRU_C = 8.0
MLA_HEADS = 16
Q_LORA = 768
KV_LORA = 512
NOPE_DIM = 128
ROPE_DIM = 64
V_DIM = 128
ROPE_THETA = 10000.0
Q_BLOCK = 128
MEM_TOKENS = 256
MEM_HEADS = 4
MEM_HEAD_DIM = D_MODEL // MEM_HEADS
MEM_Q = MEM_HEADS * MEM_HEAD_DIM
N_BRANCH = 3
D_IN = 2 * D_RNN + Q_LORA + KV_LORA + ROPE_DIM + MEM_Q + N_BRANCH * D_MODEL
N_GROUPS = 4
EXPERTS_PER_GROUP = 8
N_EXPERTS = N_GROUPS * EXPERTS_PER_GROUP
TOP_K = 2
D_EXPERT = 512

kernel_name = 'hybrid_rglru_mla_memxattn_hmoe_stream_step'


def rmsnorm(x, g):
    xf = x.astype(jnp.float32)
    y = xf * lax.rsqrt(jnp.mean(xf * xf, axis=-1, keepdims=True) + EPS)
    return (y * g.astype(jnp.float32)).astype(x.dtype)


def rope(x, pos):
    half = ROPE_DIM // 2
    inv_freq = ROPE_THETA ** (-jnp.arange(half, dtype=jnp.float32) / half)
    ang = pos.astype(jnp.float32)[:, None] * inv_freq[None, :]
    cos = jnp.cos(ang)[:, None, :]
    sin = jnp.sin(ang)[:, None, :]
    xf = x.astype(jnp.float32)
    x1, x2 = xf[..., :half], xf[..., half:]
    return jnp.concatenate([x1 * cos - x2 * sin, x1 * sin + x2 * cos], axis=-1).astype(x.dtype)


def split_in(z):
    widths = (D_RNN, D_RNN, Q_LORA, KV_LORA, ROPE_DIM, MEM_Q, N_BRANCH * D_MODEL)
    outs, off = [], 0
    for wdt in widths:
        outs.append(z[..., off:off + wdt])
        off += wdt
    return outs


def causal_conv(u, buf, w, b):
    s = u.shape[1]
    full = jnp.concatenate([buf.astype(u.dtype), u], axis=1)
    y = b + sum(full[:, k:k + s] * w[k] for k in range(CONV_W))
    return y, full[:, s:]


def _lin_combine(left, right):
    a_l, b_l = left
    a_r, b_r = right
    return a_l * a_r, a_r * b_l + b_r


def rg_lru(xc, h0, w_a, b_a, w_x, b_x, lam):
    b, s, _ = xc.shape
    xb = xc.reshape(b, s, RNN_BLOCKS, RNN_BLOCK)
    r = jax.nn.sigmoid(jnp.einsum('bsni,nij->bsnj', xb, w_a, preferred_element_type=jnp.float32).reshape(b, s, D_RNN) + b_a.astype(jnp.float32))
    i = jax.nn.sigmoid(jnp.einsum('bsni,nij->bsnj', xb, w_x, preferred_element_type=jnp.float32).reshape(b, s, D_RNN) + b_x.astype(jnp.float32))
    log_a = -LRU_C * r * jax.nn.softplus(-lam.astype(jnp.float32))
    a = jnp.exp(log_a)
    inp = jnp.sqrt(-jnp.expm1(2.0 * log_a)) * (i * xc.astype(jnp.float32))
    inp = inp.at[:, 0].add(a[:, 0] * h0.astype(jnp.float32))
    _, h = lax.associative_scan(_lin_combine, (a, inp), axis=1)
    return h.astype(xc.dtype), h[:, -1].astype(xc.dtype)


def mla_prompt_attn(q_nope, q_rope, ckv, kr, w_uk, w_uv):
    b, s, h, _ = q_nope.shape
    k_nope = jnp.einsum('bsc,chd->bshd', ckv, w_uk)
    v = jnp.einsum('bsc,chd->bshd', ckv, w_uv)
    key_chunk = jnp.arange(s) // CHUNK
    nb = s // Q_BLOCK
    qn = q_nope.reshape(b, nb, Q_BLOCK, h, NOPE_DIM).swapaxes(0, 1)
    qr = q_rope.reshape(b, nb, Q_BLOCK, h, ROPE_DIM).swapaxes(0, 1)
    scale = (NOPE_DIM + ROPE_DIM) ** -0.5

    def block(args):
        qn_b, qr_b, j = args
        sc = (jnp.einsum('bqhd,bkhd->bhqk', qn_b, k_nope, preferred_element_type=jnp.float32)
              + jnp.einsum('bqhr,bkr->bhqk', qr_b, kr, preferred_element_type=jnp.float32)) * scale
        q_chunk = (j * Q_BLOCK + jnp.arange(Q_BLOCK)) // CHUNK
        visible = key_chunk[None, :] <= q_chunk[:, None]
        p = jax.nn.softmax(jnp.where(visible, sc, -jnp.inf), axis=-1)
        return jnp.einsum('bhqk,bkhd->bqhd', p.astype(v.dtype), v)

    o = lax.map(block, (qn, qr, jnp.arange(nb)))
    return o.swapaxes(0, 1).reshape(b, s, h * V_DIM)


def mla_cached_attn(q_nope, q_rope, ckv_all, kr_all, w_uk, w_uv):
    b, s, h, _ = q_nope.shape
    scale = (NOPE_DIM + ROPE_DIM) ** -0.5
    q_lat = jnp.einsum('bqhd,chd->bqhc', q_nope, w_uk)
    sc = (jnp.einsum('bqhc,btc->bhqt', q_lat, ckv_all, preferred_element_type=jnp.float32)
          + jnp.einsum('bqhr,btr->bhqt', q_rope, kr_all, preferred_element_type=jnp.float32)) * scale
    p = jax.nn.softmax(sc, axis=-1)
    o_lat = jnp.einsum('bhqt,btc->bqhc', p.astype(ckv_all.dtype), ckv_all)
    return jnp.einsum('bqhc,chd->bqhd', o_lat, w_uv).reshape(b, s, h * V_DIM)


def mem_kv(mem, g, w_k, w_v):
    b, m, _ = mem.shape
    mn = rmsnorm(mem, g)
    k = jnp.einsum('bmd,de->bme', mn, w_k).reshape(b, m, MEM_HEADS, MEM_HEAD_DIM)
    v = jnp.einsum('bmd,de->bme', mn, w_v).reshape(b, m, MEM_HEADS, MEM_HEAD_DIM)
    return k, v


def mem_attn(q, k, v):
    b, s, _ = q.shape
    qh = q.reshape(b, s, MEM_HEADS, MEM_HEAD_DIM)
    sc = jnp.einsum('bshd,bmhd->bhsm', qh, k.astype(q.dtype), preferred_element_type=jnp.float32) * (MEM_HEAD_DIM ** -0.5)
    p = jax.nn.softmax(sc, axis=-1)
    return jnp.einsum('bhsm,bmhd->bshd', p.astype(q.dtype), v.astype(q.dtype)).reshape(b, s, MEM_Q)


def hier_moe(h, w_rg, b_rg, w_re, b_re, w_gate, w_up, w_down):
    b, s, _ = h.shape
    g_logits = jnp.einsum('bsd,dg->bsg', h, w_rg, preferred_element_type=jnp.float32) + b_rg.astype(jnp.float32)
    g_sel = jnp.argmax(g_logits, axis=-1)
    p_group = jnp.take_along_axis(jax.nn.softmax(g_logits, axis=-1), g_sel[..., None], axis=-1)
    e_logits = (jnp.einsum('bsd,de->bse', h, w_re, preferred_element_type=jnp.float32)
                + b_re.astype(jnp.float32)).reshape(b, s, N_GROUPS, EXPERTS_PER_GROUP)
    e_in = jnp.take_along_axis(e_logits, g_sel[..., None, None], axis=2)[:, :, 0]
    top_v, top_i = lax.top_k(e_in, TOP_K)
    w_top = jax.nn.softmax(top_v, axis=-1) * p_group
    eid = g_sel[..., None] * EXPERTS_PER_GROUP + top_i
    gates = jnp.einsum('bske,bsk->bse', jax.nn.one_hot(eid, N_EXPERTS, dtype=jnp.float32), w_top)
    out = jnp.zeros(h.shape, jnp.float32)
    for e in range(N_EXPERTS):
        a = jnp.einsum('bsd,df->bsf', h, w_gate[e])
        u = jnp.einsum('bsd,df->bsf', h, w_up[e])
        y = jnp.einsum('bsf,fd->bsd', jax.nn.silu(a) * u, w_down[e], preferred_element_type=jnp.float32)
        out = out + gates[..., e:e + 1] * y
    return out.astype(h.dtype)


def layer(x, pos, conv_buf, h0, mem_k, mem_v, past_ckv, past_kr, p):
    b, s, _ = x.shape
    hn = rmsnorm(x, p['norm_mix'])
    z = jnp.einsum('bsd,de->bse', hn, p['w_in'])
    u_x, u_y, c_q, c_kv, k_r, q_m, g_pre = split_in(z)
    xc, new_conv = causal_conv(u_x, conv_buf, p['conv_w'], p['conv_b'])
    hr, h_last = rg_lru(xc, h0, p['lru_wa'], p['lru_ba'], p['lru_wx'], p['lru_bx'], p['lru_lambda'])
    br_rec = jnp.einsum('bsc,cd->bsd', hr * jax.nn.gelu(u_y), p['w_branch_rec'])
    q = jnp.einsum('bsc,che->bshe', rmsnorm(c_q, p['mla_q_norm']), p['mla_w_uq'])
    q_nope = q[..., :NOPE_DIM]
    q_rope = rope(q[..., NOPE_DIM:], pos)
    ckv = rmsnorm(c_kv, p['mla_kv_norm'])
    kr = rope(k_r[:, :, None, :], pos)[:, :, 0, :]
    if past_ckv is None:
        o = mla_prompt_attn(q_nope, q_rope, ckv, kr, p['mla_w_uk'], p['mla_w_uv'])
    else:
        ckv_all = jnp.concatenate([past_ckv.astype(ckv.dtype), ckv], axis=1)
        kr_all = jnp.concatenate([past_kr.astype(kr.dtype), kr], axis=1)
        o = mla_cached_attn(q_nope, q_rope, ckv_all, kr_all, p['mla_w_uk'], p['mla_w_uv'])
    br_mla = jnp.einsum('bse,ed->bsd', o, p['w_branch_mla'])
    br_mem = jnp.einsum('bse,ed->bsd', mem_attn(q_m, mem_k, mem_v), p['w_branch_mem'])
    g = jax.nn.sigmoid(g_pre.astype(jnp.float32)).reshape(b, s, N_BRANCH, D_MODEL).astype(x.dtype)
    merged = g[:, :, 0] * br_rec + g[:, :, 1] * br_mla + g[:, :, 2] * br_mem
    x = x + jnp.einsum('bsd,de->bse', merged, p['w_out'])
    x = x + hier_moe(rmsnorm(x, p['norm_ffn']), p['router_group_w'], p['router_group_b'],
                     p['router_expert_w'], p['router_expert_b'], p['moe_w_gate'], p['moe_w_up'], p['moe_w_down'])
    return x, new_conv, h_last, ckv, kr


def setup_inputs(seed: int = 0) -> dict:
    key = jax.random.key(seed)
    ks = iter(jax.random.split(key, 48))

    def nrm(shape, scale):
        return jax.random.normal(next(ks), shape, jnp.float32) * scale

    def gain(shape):
        return 1.0 + nrm(shape, 0.02)

    a0 = jax.random.uniform(next(ks), (DEPTH, D_RNN), jnp.float32, minval=0.9, maxval=0.999)
    lru_lambda = jnp.log(a0) - jnp.log1p(-a0)
    return {
        'x_prompt': nrm((BATCH, SEQ, D_MODEL), 1.0),
        'x_sample': nrm((DEC_BATCH, DEC_SEQ, D_MODEL), 1.0),
        'mem_prompt': nrm((BATCH, MEM_TOKENS, D_MODEL), 1.0),
        'cache_mla_ckv': nrm((DEPTH, DEC_BATCH, PAST_LEN, KV_LORA), 1.0),
        'cache_mla_krope': nrm((DEPTH, DEC_BATCH, PAST_LEN, ROPE_DIM), 1.0),
        'cache_mem_k': nrm((DEPTH, DEC_BATCH, MEM_TOKENS, MEM_HEADS, MEM_HEAD_DIM), 1.0),
        'cache_mem_v': nrm((DEPTH, DEC_BATCH, MEM_TOKENS, MEM_HEADS, MEM_HEAD_DIM), 1.0),
        'state_rglru_h': nrm((DEPTH, DEC_BATCH, D_RNN), 1.0),
        'state_rglru_conv': nrm((DEPTH, DEC_BATCH, CONV_W - 1, D_RNN), 1.0),
        'norm_mix': gain((DEPTH, D_MODEL)),
        'w_in': nrm((DEPTH, D_MODEL, D_IN), D_MODEL ** -0.5),
        'conv_w': nrm((DEPTH, CONV_W, D_RNN), CONV_W ** -0.5),
        'conv_b': nrm((DEPTH, D_RNN), 0.01),
        'lru_wa': nrm((DEPTH, RNN_BLOCKS, RNN_BLOCK, RNN_BLOCK), RNN_BLOCK ** -0.5),
        'lru_ba': nrm((DEPTH, D_RNN), 0.01),
        'lru_wx': nrm((DEPTH, RNN_BLOCKS, RNN_BLOCK, RNN_BLOCK), RNN_BLOCK ** -0.5),
        'lru_bx': nrm((DEPTH, D_RNN), 0.01),
        'lru_lambda': lru_lambda,
        'mla_q_norm': gain((DEPTH, Q_LORA)),
        'mla_w_uq': nrm((DEPTH, Q_LORA, MLA_HEADS, NOPE_DIM + ROPE_DIM), Q_LORA ** -0.5),
        'mla_kv_norm': gain((DEPTH, KV_LORA)),
        'mla_w_uk': nrm((DEPTH, KV_LORA, MLA_HEADS, NOPE_DIM), KV_LORA ** -0.5),
        'mla_w_uv': nrm((DEPTH, KV_LORA, MLA_HEADS, V_DIM), KV_LORA ** -0.5),
        'mem_norm': gain((DEPTH, D_MODEL)),
        'mem_w_k': nrm((DEPTH, D_MODEL, MEM_Q), D_MODEL ** -0.5),
        'mem_w_v': nrm((DEPTH, D_MODEL, MEM_Q), D_MODEL ** -0.5),
        'w_branch_rec': nrm((DEPTH, D_RNN, D_MODEL), D_RNN ** -0.5),
        'w_branch_mla': nrm((DEPTH, MLA_HEADS * V_DIM, D_MODEL), (MLA_HEADS * V_DIM) ** -0.5),
        'w_branch_mem': nrm((DEPTH, MEM_Q, D_MODEL), MEM_Q ** -0.5),
        'w_out': nrm((DEPTH, D_MODEL, D_MODEL), D_MODEL ** -0.5),
        'norm_ffn': gain((DEPTH, D_MODEL)),
        'router_group_w': nrm((DEPTH, D_MODEL, N_GROUPS), D_MODEL ** -0.5),
        'router_group_b': nrm((DEPTH, N_GROUPS), 0.01),
        'router_expert_w': nrm((DEPTH, D_MODEL, N_EXPERTS), D_MODEL ** -0.5),
        'router_expert_b': nrm((DEPTH, N_EXPERTS), 0.01),
        'moe_w_gate': nrm((DEPTH, N_EXPERTS, D_MODEL, D_EXPERT), D_MODEL ** -0.5),
        'moe_w_up': nrm((DEPTH, N_EXPERTS, D_MODEL, D_EXPERT), D_MODEL ** -0.5),
        'moe_w_down': nrm((DEPTH, N_EXPERTS, D_EXPERT, D_MODEL), D_EXPERT ** -0.5),
        'norm_final': gain((D_MODEL,)),
    }


def reference(x_prompt, x_sample, mem_prompt, cache_mla_ckv, cache_mla_krope, cache_mem_k, cache_mem_v,
              state_rglru_h, state_rglru_conv, norm_mix, w_in, conv_w, conv_b, lru_wa, lru_ba, lru_wx, lru_bx,
              lru_lambda, mla_q_norm, mla_w_uq, mla_kv_norm, mla_w_uk, mla_w_uv, mem_norm, mem_w_k, mem_w_v,
              w_branch_rec, w_branch_mla, w_branch_mem, w_out, norm_ffn, router_group_w, router_group_b,
              router_expert_w, router_expert_b, moe_w_gate, moe_w_up, moe_w_down, norm_final):
    bp, sp, _ = x_prompt.shape
    pos_p = jnp.arange(sp)
    pos_s = PAST_LEN + jnp.arange(x_sample.shape[1])
    conv0 = jnp.zeros((bp, CONV_W - 1, D_RNN), x_prompt.dtype)
    h00 = jnp.zeros((bp, D_RNN), x_prompt.dtype)
    xp, xs = x_prompt, x_sample
    p_ckv, p_kr, p_mk, p_mv, p_h, p_conv = [], [], [], [], [], []
    s_ckv, s_kr, s_h, s_conv = [], [], [], []
    for l in range(DEPTH):
        prm = dict(norm_mix=norm_mix[l], w_in=w_in[l], conv_w=conv_w[l], conv_b=conv_b[l],
                   lru_wa=lru_wa[l], lru_ba=lru_ba[l], lru_wx=lru_wx[l], lru_bx=lru_bx[l],
                   lru_lambda=lru_lambda[l], mla_q_norm=mla_q_norm[l], mla_w_uq=mla_w_uq[l],
                   mla_kv_norm=mla_kv_norm[l], mla_w_uk=mla_w_uk[l], mla_w_uv=mla_w_uv[l],
                   w_branch_rec=w_branch_rec[l], w_branch_mla=w_branch_mla[l], w_branch_mem=w_branch_mem[l],
                   w_out=w_out[l], norm_ffn=norm_ffn[l], router_group_w=router_group_w[l],
                   router_group_b=router_group_b[l], router_expert_w=router_expert_w[l],
                   router_expert_b=router_expert_b[l], moe_w_gate=moe_w_gate[l], moe_w_up=moe_w_up[l],
                   moe_w_down=moe_w_down[l])
        mk, mv = mem_kv(mem_prompt, mem_norm[l], mem_w_k[l], mem_w_v[l])
        xp, cb, hl, ckv, kr = layer(xp, pos_p, conv0, h00, mk, mv, None, None, prm)
        p_ckv.append(ckv); p_kr.append(kr); p_mk.append(mk); p_mv.append(mv); p_h.append(hl); p_conv.append(cb)
        xs, cb_s, hl_s, ckv_s, kr_s = layer(xs, pos_s, state_rglru_conv[l], state_rglru_h[l],
                                            cache_mem_k[l], cache_mem_v[l], cache_mla_ckv[l],
                                            cache_mla_krope[l], prm)
        s_ckv.append(ckv_s); s_kr.append(kr_s); s_h.append(hl_s); s_conv.append(cb_s)
    y_prompt = rmsnorm(xp, norm_final)
    y_sample = rmsnorm(xs, norm_final)
    return (y_prompt, y_sample,
            jnp.stack(p_ckv), jnp.stack(p_kr), jnp.stack(p_mk), jnp.stack(p_mv), jnp.stack(p_h), jnp.stack(p_conv),
            jnp.stack(s_ckv), jnp.stack(s_kr), jnp.stack(s_h), jnp.stack(s_conv))
```

```python
import functools

import jax
import jax.numpy as jnp
from jax import lax
from jax.experimental import pallas as pl
from jax.experimental.pallas import tpu as pltpu

F32 = jnp.float32
BF16 = jnp.bfloat16

EPS = 1e-6
CHUNK = 64
LRU_C = 8.0
ROPE_THETA = 10000.0
CONV_W = 4
NOPE_DIM = 128
ROPE_DIM = 64
V_DIM = 128
N_GROUPS = 4
TOP_K = 2

LANES = 128
SUBLANES = 8
HEAD_PAD = 2 * LANES
V7X_VMEM_BYTES = 64 * 1024 * 1024
VMEM_LIMIT = (V7X_VMEM_BYTES * 7) // 8
NEG_BIG = -1e30


def _params(*sem):
    return pltpu.CompilerParams(dimension_semantics=sem, vmem_limit_bytes=VMEM_LIMIT)


def _tile(n, pref, mult=SUBLANES):
    t = min(pref, n)
    t -= t % mult
    while n % t:
        t -= mult
    return t


def _rms(x, g):
    y = x * lax.rsqrt(jnp.mean(x * x, axis=-1, keepdims=True) + EPS)
    return y * g


def _gelu(x):
    return 0.5 * x * (1.0 + jnp.tanh(0.7978845608028654 * (x + 0.044715 * (x * x * x))))


def _sigmoid(x):
    return 1.0 / (1.0 + jnp.exp(-x))


def _dot(a, b):
    return jnp.dot(a, b, preferred_element_type=F32)


def _dot_t(a, b):
    return lax.dot_general(a, b, (((1,), (1,)), ((), ())), preferred_element_type=F32)


def _norm_matmul_body(x_ref, g_ref, w_ref, z_ref, xn_ref, xn_scr):
    @pl.when(pl.program_id(1) == 0)
    def _():
        xn = _rms(x_ref[...], g_ref[...]).astype(BF16)
        xn_scr[...] = xn
        xn_ref[...] = xn

    z_ref[...] = _dot(xn_scr[...], w_ref[...]).astype(z_ref.dtype)


def norm_matmul(x, g, w, out_dtype, tm_pref=512, tn_pref=1024):
    t, d = x.shape
    n = w.shape[1]
    tm, tn = _tile(t, tm_pref, 16), _tile(n, tn_pref, LANES)
    return pl.pallas_call(
        _norm_matmul_body,
        grid=(t // tm, n // tn),
        in_specs=[
            pl.BlockSpec((tm, d), lambda i, j: (i, 0)),
            pl.BlockSpec((1, d), lambda i, j: (0, 0)),
            pl.BlockSpec((d, tn), lambda i, j: (0, j)),
        ],
        out_specs=[
            pl.BlockSpec((tm, tn), lambda i, j: (i, j)),
            pl.BlockSpec((tm, d), lambda i, j: (i, 0)),
        ],
        out_shape=[jax.ShapeDtypeStruct((t, n), out_dtype), jax.ShapeDtypeStruct((t, d), BF16)],
        scratch_shapes=[pltpu.VMEM((tm, d), BF16)],
        compiler_params=_params("arbitrary", "arbitrary"),
        name="norm_matmul",
    )(x, g, w)


def _rope_mix(r, c, s):
    return r * c + pltpu.roll(r, ROPE_DIM, 1) * s


def _q_proj_body(xn_ref, wq_ref, gq_ref, wuq_ref, cos_ref, sin_ref, q_ref, *, n_heads, scale):
    cq = _dot(xn_ref[...], wq_ref[...])
    cqn = _rms(cq, gq_ref[...]).astype(BF16)
    c = cos_ref[...] * scale
    s = sin_ref[...] * scale
    for h in range(n_heads):
        qh = _dot(cqn, wuq_ref[:, h * HEAD_PAD:(h + 1) * HEAD_PAD])
        q_ref[h, :, :NOPE_DIM] = (qh[:, :NOPE_DIM] * scale).astype(BF16)
        q_ref[h, :, NOPE_DIM:] = _rope_mix(qh[:, NOPE_DIM:], c, s).astype(BF16)


def q_proj(xn, wq, gq, wuq, cos, sin, n_heads, scale, tm_pref=512):
    t, d = xn.shape
    ql = wq.shape[1]
    tm = _tile(t, tm_pref, 16)
    return pl.pallas_call(
        functools.partial(_q_proj_body, n_heads=n_heads, scale=scale),
        grid=(t // tm,),
        in_specs=[
            pl.BlockSpec((tm, d), lambda i: (i, 0)),
            pl.BlockSpec((d, ql), lambda i: (0, 0)),
            pl.BlockSpec((1, ql), lambda i: (0, 0)),
            pl.BlockSpec((ql, n_heads * HEAD_PAD), lambda i: (0, 0)),
            pl.BlockSpec((tm, LANES), lambda i: (i, 0)),
            pl.BlockSpec((tm, LANES), lambda i: (i, 0)),
        ],
        out_specs=pl.BlockSpec((n_heads, tm, HEAD_PAD), lambda i: (0, i, 0)),
        out_shape=jax.ShapeDtypeStruct((n_heads, t, HEAD_PAD), BF16),
        compiler_params=_params("arbitrary"),
        name="q_proj",
    )(xn, wq, gq, wuq, cos, sin)


def _kv_proj_body(xn_ref, wkv_ref, gkv_ref, wuk_ref, wuv_ref, cos_ref, sin_ref,
                  ckv_ref, kr_ref, k_ref, v_ref, *, n_heads, kv_lora):
    ckr = _dot(xn_ref[...], wkv_ref[...])
    ckv = _rms(ckr[:, :kv_lora], gkv_ref[...])
    ckv_ref[...] = ckv
    rr = _rope_mix(ckr[:, kv_lora:], cos_ref[...], sin_ref[...])
    kr_ref[...] = rr[:, :ROPE_DIM]
    ckv_b = ckv.astype(BF16)
    rr_b = rr.astype(BF16)
    kn = _dot(ckv_b, wuk_ref[...])
    vv = _dot(ckv_b, wuv_ref[...])
    for h in range(n_heads):
        k_ref[h, :, :NOPE_DIM] = kn[:, h * NOPE_DIM:(h + 1) * NOPE_DIM].astype(BF16)
        k_ref[h, :, NOPE_DIM:] = rr_b
        v_ref[h] = vv[:, h * V_DIM:(h + 1) * V_DIM].astype(BF16)


def kv_proj(xn, wkv, gkv, wuk, wuv, cos, sin, n_heads, tm_pref=512):
    t, d = xn.shape
    kv_lora = gkv.shape[1]
    tm = _tile(t, tm_pref, 16)
    return pl.pallas_call(
        functools.partial(_kv_proj_body, n_heads=n_heads, kv_lora=kv_lora),
        grid=(t // tm,),
        in_specs=[
            pl.BlockSpec((tm, d), lambda i: (i, 0)),
            pl.BlockSpec((d, kv_lora + LANES), lambda i: (0, 0)),
            pl.BlockSpec((1, kv_lora), lambda i: (0, 0)),
            pl.BlockSpec((kv_lora, n_heads * NOPE_DIM), lambda i: (0, 0)),
            pl.BlockSpec((kv_lora, n_heads * V_DIM), lambda i: (0, 0)),
            pl.BlockSpec((tm, LANES), lambda i: (i, 0)),
            pl.BlockSpec((tm, LANES), lambda i: (i, 0)),
        ],
        out_specs=[
            pl.BlockSpec((tm, kv_lora), lambda i: (i, 0)),
            pl.BlockSpec((tm, ROPE_DIM), lambda i: (i, 0)),
            pl.BlockSpec((n_heads, tm, HEAD_PAD), lambda i: (0, i, 0)),
            pl.BlockSpec((n_heads, tm, V_DIM), lambda i: (0, i, 0)),
        ],
        out_shape=[
            jax.ShapeDtypeStruct((t, kv_lora), F32),
            jax.ShapeDtypeStruct((t, ROPE_DIM), F32),
            jax.ShapeDtypeStruct((n_heads, t, HEAD_PAD), BF16),
            jax.ShapeDtypeStruct((n_heads, t, V_DIM), BF16),
        ],
        compiler_params=_params("arbitrary"),
        name="kv_proj",
    )(xn, wkv, gkv, wuk, wuv, cos, sin)


def _rglru_body(ux_ref, uy_ref, conv0_ref, h0_ref, cw_ref, cb_ref, wa_ref, wx_ref, ba_ref, bx_ref,
                lam_ref, o_ref, hlast_ref, nconv_ref, ubuf, a_scr, b_scr, hcar, *, ts, n_blocks):
    hist = SUBLANES
    n_col = a_scr.shape[0]
    d = n_col * LANES
    col = lambda c: slice(c * LANES, (c + 1) * LANES)

    @pl.when(pl.program_id(1) == 0)
    def _():
        ubuf[hist - (CONV_W - 1):hist, :] = conv0_ref[0]
        hcar[...] = h0_ref[0]

    ubuf[hist:hist + ts, :] = ux_ref[...].astype(F32)
    cw = cw_ref[...]
    xc = cb_ref[...]
    for k in range(CONV_W):
        off = hist - (CONV_W - 1) + k
        xc = xc + ubuf[off:off + ts, :] * cw[k:k + 1, :]
    tail = ubuf[hist + ts - (CONV_W - 1):hist + ts, :]
    nconv_ref[0] = tail
    ubuf[hist - (CONV_W - 1):hist, :] = tail

    lam = lam_ref[...]
    sp = jnp.maximum(-lam, 0.0) + jnp.log(1.0 + jnp.exp(-jnp.abs(lam)))
    for n in range(n_blocks):
        sl = col(n)
        xcn = xc[:, sl]
        xb = xcn.astype(BF16)
        r = _sigmoid(_dot(xb, wa_ref[n]) + ba_ref[:, sl])
        i = _sigmoid(_dot(xb, wx_ref[n]) + bx_ref[:, sl])
        a = jnp.exp((-LRU_C) * r * sp[:, sl])
        a_scr[n] = a
        b_scr[n] = jnp.sqrt(1.0 - a * a) * (i * xcn)

    seg = ts // SUBLANES

    def step(t, carry):
        h, p = carry
        rows = pl.ds(t, SUBLANES, stride=seg)
        hs, ps = [], []
        for c in range(n_col):
            a = a_scr[c, rows, :]
            hc = a * h[:, col(c)] + b_scr[c, rows, :]
            pc = a * p[:, col(c)]
            b_scr[c, rows, :] = hc
            a_scr[c, rows, :] = pc
            hs.append(hc)
            ps.append(pc)
        return jnp.concatenate(hs, axis=1), jnp.concatenate(ps, axis=1)

    h, p = lax.fori_loop(0, seg, step, (jnp.zeros((SUBLANES, d), F32), jnp.ones((SUBLANES, d), F32)))
    carry_in = hcar[...]
    for s in range(SUBLANES):
        rows = slice(s * seg, (s + 1) * seg)
        for c in range(n_col):
            b_scr[c, rows, :] = b_scr[c, rows, :] + a_scr[c, rows, :] * carry_in[:, col(c)]
        carry_in = h[s:s + 1, :] + p[s:s + 1, :] * carry_in
    hcar[...] = carry_in
    hlast_ref[0] = carry_in
    for c in range(n_col):
        o_ref[:, col(c)] = (b_scr[c] * _gelu(uy_ref[:, col(c)].astype(F32))).astype(o_ref.dtype)


def rglru(z, row0, n_batch, seq, conv0, h0, cw, cb, wa, wx, ba, bx, lam, d_rnn, ts_pref=256):
    ts = _tile(seq, ts_pref, 16)
    nst = seq // ts
    blk0 = row0 // ts
    n_blocks = wa.shape[0]
    assert d_rnn == n_blocks * LANES and ts % SUBLANES == 0
    vec = pl.BlockSpec((1, d_rnn), lambda b, s: (0, 0))
    in_specs = [
        pl.BlockSpec((ts, d_rnn), lambda b, s: (blk0 + b * nst + s, 0)),
        pl.BlockSpec((ts, d_rnn), lambda b, s: (blk0 + b * nst + s, 1)),
        pl.BlockSpec((1, CONV_W - 1, d_rnn), lambda b, s: (b, 0, 0)),
        pl.BlockSpec((1, 1, d_rnn), lambda b, s: (b, 0, 0)),
        pl.BlockSpec((CONV_W, d_rnn), lambda b, s: (0, 0)),
        vec,
        pl.BlockSpec(wa.shape, lambda b, s: (0, 0, 0)),
        pl.BlockSpec(wx.shape, lambda b, s: (0, 0, 0)),
        vec, vec, vec,
    ]
    return pl.pallas_call(
        functools.partial(_rglru_body, ts=ts, n_blocks=n_blocks),
        grid=(n_batch, nst),
        in_specs=in_specs,
        out_specs=[
            pl.BlockSpec((ts, d_rnn), lambda b, s: (b * nst + s, 0)),
            pl.BlockSpec((1, 1, d_rnn), lambda b, s: (b, 0, 0)),
            pl.BlockSpec((1, CONV_W - 1, d_rnn), lambda b, s: (b, 0, 0)),
        ],
        out_shape=[
            jax.ShapeDtypeStruct((n_batch * seq, d_rnn), BF16),
            jax.ShapeDtypeStruct((n_batch, 1, d_rnn), F32),
            jax.ShapeDtypeStruct((n_batch, CONV_W - 1, d_rnn), F32),
        ],
        scratch_shapes=[
            pltpu.VMEM((SUBLANES + ts, d_rnn), F32),
            pltpu.VMEM((d_rnn // LANES, ts, LANES), F32),
            pltpu.VMEM((d_rnn // LANES, ts, LANES), F32),
            pltpu.VMEM((1, d_rnn), F32),
        ],
        compiler_params=_params("arbitrary", "arbitrary"),
        name="rglru",
    )(z, z, conv0, h0, cw, cb, wa, wx, ba, bx, lam)


def _softmax_step(s, v, m, l, acc):
    m_new = jnp.maximum(m, jnp.max(s, axis=-1, keepdims=True))
    alpha = jnp.exp(m - m_new)
    p = jnp.exp(s - m_new)
    l = alpha * l + jnp.sum(p, axis=-1, keepdims=True)
    acc = alpha * acc + _dot(p.astype(BF16), v)
    return m_new, l, acc


def _attn_prompt_body(q_ref, k_ref, v_ref, o_ref, *, tq):
    qi = pl.program_id(2)
    q = q_ref[0]

    def kv(j):
        off = pl.multiple_of(j * tq, tq)
        return k_ref[0, pl.ds(off, tq), :], v_ref[0, pl.ds(off, tq), :]

    def body(j, carry):
        k, v = kv(j)
        return _softmax_step(_dot_t(q, k), v, *carry)

    init = (jnp.full((tq, 1), NEG_BIG, F32), jnp.zeros((tq, 1), F32), jnp.zeros((tq, V_DIM), F32))
    m, l, acc = lax.fori_loop(0, qi, body, init)
    k, v = kv(qi)
    s = _dot_t(q, k)
    shift = CHUNK.bit_length() - 1
    row_chunk = lax.broadcasted_iota(jnp.int32, (tq, tq), 0) >> shift
    col_chunk = lax.broadcasted_iota(jnp.int32, (tq, tq), 1) >> shift
    s = jnp.where(col_chunk <= row_chunk, s, NEG_BIG)
    m, l, acc = _softmax_step(s, v, m, l, acc)
    o_ref[...] = (acc / l).astype(o_ref.dtype)


def attn_prompt(q, k, v, n_batch, seq, tq_pref=512):
    n_heads = q.shape[0]
    tq = _tile(seq, tq_pref, CHUNK)
    nq = seq // tq
    return pl.pallas_call(
        functools.partial(_attn_prompt_body, tq=tq),
        grid=(n_batch, n_heads, nq),
        in_specs=[
            pl.BlockSpec((1, tq, HEAD_PAD), lambda b, h, i: (h, b * nq + i, 0)),
            pl.BlockSpec((1, seq, HEAD_PAD), lambda b, h, i: (h, b, 0)),
            pl.BlockSpec((1, seq, V_DIM), lambda b, h, i: (h, b, 0)),
        ],
        out_specs=pl.BlockSpec((tq, V_DIM), lambda b, h, i: (b * nq + i, h)),
        out_shape=jax.ShapeDtypeStruct((n_batch * seq, n_heads * V_DIM), BF16),
        compiler_params=_params("arbitrary", "arbitrary", "arbitrary"),
        name="attn_prompt",
    )(q, k, v)


def _attn_cached_body(q_ref, ckvn_ref, krn_ref, cckv_ref, ckr_ref, wukt_ref, wuv_ref, o_ref,
                      ql_scr, qr_scr, *, n_heads, sq, tk, past):
    for h in range(n_heads):
        qh = q_ref[h]
        ql_scr[h * sq:(h + 1) * sq, :] = _dot(qh[:, :NOPE_DIM], wukt_ref[h]).astype(BF16)
        qr_scr[h * sq:(h + 1) * sq, :] = qh[:, NOPE_DIM:NOPE_DIM + ROPE_DIM]
    ql = ql_scr[...]
    qr = qr_scr[...]
    rows = n_heads * sq
    kv_lora = ql.shape[1]

    def body(j, carry):
        off = pl.multiple_of(j * tk, tk)
        kc = cckv_ref[0, 0, pl.ds(off, tk), :].astype(BF16)
        kr = ckr_ref[0, 0, pl.ds(off, tk), :].astype(BF16)
        return _softmax_step(_dot_t(ql, kc) + _dot_t(qr, kr), kc, *carry)

    init = (jnp.full((rows, 1), NEG_BIG, F32), jnp.zeros((rows, 1), F32), jnp.zeros((rows, kv_lora), F32))
    carry = lax.fori_loop(0, past // tk, body, init)
    kc = ckvn_ref[...].astype(BF16)
    kr = krn_ref[...].astype(BF16)
    m, l, acc = _softmax_step(_dot_t(ql, kc) + _dot_t(qr, kr), kc, *carry)
    o_lat = (acc / l).astype(BF16)
    for h in range(n_heads):
        o_ref[:, h * V_DIM:(h + 1) * V_DIM] = _dot(o_lat[h * sq:(h + 1) * sq, :], wuv_ref[h]).astype(o_ref.dtype)


def attn_cached(q, ckv, kr, cache_ckv, cache_kr, layer, wukt, wuv, row0, n_batch, sq, tk_pref=512):
    n_heads = q.shape[0]
    kv_lora = ckv.shape[1]
    past = cache_ckv.shape[2]
    tk = _tile(past, tk_pref, 16)
    blk0 = row0 // sq
    return pl.pallas_call(
        functools.partial(_attn_cached_body, n_heads=n_heads, sq=sq, tk=tk, past=past),
        grid=(n_batch,),
        in_specs=[
            pl.BlockSpec((n_heads, sq, HEAD_PAD), lambda b: (0, blk0 + b, 0)),
            pl.BlockSpec((sq, kv_lora), lambda b: (blk0 + b, 0)),
            pl.BlockSpec((sq, ROPE_DIM), lambda b: (blk0 + b, 0)),
            pl.BlockSpec((1, 1, past, kv_lora), lambda b: (layer, b, 0, 0)),
            pl.BlockSpec((1, 1, past, ROPE_DIM), lambda b: (layer, b, 0, 0)),
            pl.BlockSpec(wukt.shape, lambda b: (0, 0, 0)),
            pl.BlockSpec(wuv.shape, lambda b: (0, 0, 0)),
        ],
        out_specs=pl.BlockSpec((sq, n_heads * V_DIM), lambda b: (b, 0)),
        out_shape=jax.ShapeDtypeStruct((n_batch * sq, n_heads * V_DIM), BF16),
        scratch_shapes=[pltpu.VMEM((n_heads * sq, kv_lora), BF16), pltpu.VMEM((n_heads * sq, ROPE_DIM), BF16)],
        compiler_params=_params("arbitrary"),
        name="attn_cached",
    )(q, ckv, kr, cache_ckv, cache_kr, wukt, wuv)


def _mem_attn_body(q_ref, k_ref, v_ref, o_ref, *, n_heads):
    hd = q_ref.shape[1] // n_heads
    scale = hd ** -0.5
    for h in range(n_heads):
        sl = slice(h * hd, (h + 1) * hd)
        s = _dot_t(q_ref[:, sl], k_ref[0, 0, :, sl].astype(BF16)) * scale
        e = jnp.exp(s - jnp.max(s, axis=-1, keepdims=True))
        o = _dot(e.astype(BF16), v_ref[0, 0, :, sl].astype(BF16)) / jnp.sum(e, axis=-1, keepdims=True)
        o_ref[:, sl] = o.astype(o_ref.dtype)


def mem_attn(z, col_blk, mem_k, mem_v, layer, row0, n_batch, seq, n_heads, tm_pref=512):
    _, _, m_tok, dq = mem_k.shape
    tm = _tile(seq, tm_pref, 16)
    nst = seq // tm
    blk0 = row0 // tm
    return pl.pallas_call(
        functools.partial(_mem_attn_body, n_heads=n_heads),
        grid=(n_batch, nst),
        in_specs=[
            pl.BlockSpec((tm, dq), lambda b, s: (blk0 + b * nst + s, col_blk)),
            pl.BlockSpec((1, 1, m_tok, dq), lambda b, s: (layer, b, 0, 0)),
            pl.BlockSpec((1, 1, m_tok, dq), lambda b, s: (layer, b, 0, 0)),
        ],
        out_specs=pl.BlockSpec((tm, dq), lambda b, s: (b * nst + s, 0)),
        out_shape=jax.ShapeDtypeStruct((n_batch * seq, dq), BF16),
        compiler_params=_params("arbitrary", "arbitrary"),
        name="mem_attn",
    )(z, mem_k, mem_v)


def _merge_body(xn_ref, p0_ref, p1_ref, p2_ref, s0_ref, s1_ref, s2_ref,
                g0_ref, g1_ref, g2_ref, w0_ref, w1_ref, w2_ref, o_ref, *, n_first):
    def emit(a0_ref, a1_ref, a2_ref):
        xn = xn_ref[...]
        acc = _sigmoid(_dot(xn, g0_ref[...])) * _dot(a0_ref[...], w0_ref[...])
        acc = acc + _sigmoid(_dot(xn, g1_ref[...])) * _dot(a1_ref[...], w1_ref[...])
        acc = acc + _sigmoid(_dot(xn, g2_ref[...])) * _dot(a2_ref[...], w2_ref[...])
        o_ref[...] = acc.astype(o_ref.dtype)

    i = pl.program_id(1)

    @pl.when(i < n_first)
    def _():
        emit(p0_ref, p1_ref, p2_ref)

    @pl.when(i >= n_first)
    def _():
        emit(s0_ref, s1_ref, s2_ref)


def merge_branches(xn, first, second, wg, w0, w1, w2, tm_pref=256, tn_pref=512):
    t, d = xn.shape
    t1, t2 = first[0].shape[0], second[0].shape[0]
    tm, tn = _tile(min(t1, t2), tm_pref, 16), _tile(d, tn_pref, LANES)
    assert t1 % tm == 0 and t2 % tm == 0 and t1 + t2 == t
    n1 = t1 // tm
    nj = d // tn
    act = pl.BlockSpec((tm, d), lambda j, i: (i, 0))
    act1 = pl.BlockSpec((tm, d), lambda j, i: (jnp.minimum(i, n1 - 1), 0))
    act2 = pl.BlockSpec((tm, d), lambda j, i: (jnp.maximum(i - n1, 0), 0))
    wsp = pl.BlockSpec((d, tn), lambda j, i: (0, j))
    return pl.pallas_call(
        functools.partial(_merge_body, n_first=n1),
        grid=(nj, t // tm),
        in_specs=[act, act1, act1, act1, act2, act2, act2,
                  pl.BlockSpec((d, tn), lambda j, i: (0, j)),
                  pl.BlockSpec((d, tn), lambda j, i: (0, nj + j)),
                  pl.BlockSpec((d, tn), lambda j, i: (0, 2 * nj + j)),
                  wsp, wsp, wsp],
        out_specs=pl.BlockSpec((tm, tn), lambda j, i: (i, j)),
        out_shape=jax.ShapeDtypeStruct((t, d), BF16),
        compiler_params=_params("arbitrary", "arbitrary"),
        name="merge_branches",
    )(xn, *first, *second, wg, wg, wg, w0, w1, w2)


def _out_router_body(x_ref, m_ref, wo_ref, gf_ref, wr_ref, wrl_ref, br_ref, x1_ref, route_ref, *, n_experts):
    x1 = x_ref[...] + _dot(m_ref[...], wo_ref[...])
    x1_ref[...] = x1
    hn = _rms(x1, gf_ref[...])
    hn_hi = hn.astype(BF16)
    hn_lo = (hn - hn_hi.astype(F32)).astype(BF16)
    lg = (_dot(hn_hi, wr_ref[...]) + (_dot(hn_lo, wr_ref[...]) + _dot(hn_hi, wrl_ref[...]))) + br_ref[...]
    lane = lax.broadcasted_iota(jnp.int32, lg.shape, 1).astype(F32)
    epg = n_experts // N_GROUPS
    is_g = lane < N_GROUPS
    gl = jnp.where(is_g, lg, -jnp.inf)
    gmax = jnp.max(gl, axis=-1, keepdims=True)
    g_sel = jnp.min(jnp.where(gl == gmax, lane, LANES), axis=-1, keepdims=True)
    pg = 1.0 / jnp.sum(jnp.where(is_g, jnp.exp(lg - gmax), 0.0), axis=-1, keepdims=True)
    lo = N_GROUPS + g_sel * epg
    el = jnp.where((lane >= lo) & (lane < lo + epg), lg, -jnp.inf)
    t1 = jnp.max(el, axis=-1, keepdims=True)
    i1 = jnp.min(jnp.where(el == t1, lane, LANES), axis=-1, keepdims=True)
    el2 = jnp.where(lane == i1, -jnp.inf, el)
    t2 = jnp.max(el2, axis=-1, keepdims=True)
    i2 = jnp.min(jnp.where(el2 == t2, lane, LANES), axis=-1, keepdims=True)
    dlt = jnp.exp(t2 - t1)
    w1 = pg / (1.0 + dlt)
    w2 = pg * dlt / (1.0 + dlt)
    e1 = i1 - N_GROUPS
    e2 = i2 - N_GROUPS
    route = jnp.where(lane == 0, e1, jnp.where(lane == 1, e2, jnp.where(lane == 2, w1, jnp.where(lane == 3, w2, 0.0))))
    route_ref[...] = route


def out_router(x, merged, wo, gf, wr, wr_lo, br, n_experts, tm_pref=512):
    t, d = x.shape
    tm = _tile(t, tm_pref, 16)
    row = pl.BlockSpec((tm, d), lambda i: (i, 0))
    return pl.pallas_call(
        functools.partial(_out_router_body, n_experts=n_experts),
        grid=(t // tm,),
        in_specs=[row, row,
                  pl.BlockSpec((d, d), lambda i: (0, 0)),
                  pl.BlockSpec((1, d), lambda i: (0, 0)),
                  pl.BlockSpec((d, LANES), lambda i: (0, 0)),
                  pl.BlockSpec((d, LANES), lambda i: (0, 0)),
                  pl.BlockSpec((1, LANES), lambda i: (0, 0))],
        out_specs=[row, pl.BlockSpec((tm, LANES), lambda i: (i, 0))],
        out_shape=[jax.ShapeDtypeStruct((t, d), F32), jax.ShapeDtypeStruct((t, LANES), F32)],
        compiler_params=_params("arbitrary"),
        name="out_router",
    )(x, merged, wo, gf, wr, wr_lo, br)


def _row_copy(src_hbm, dst_hbm, sem, src_row, dst_row, n_rows):
    return pltpu.make_async_copy(src_hbm.at[pl.ds(src_row, n_rows)], dst_hbm.at[pl.ds(dst_row, n_rows)], sem)


def _gather_body(src_ref, x_hbm, o_hbm, sem, *, rows):
    base = pl.program_id(0) * rows

    def issue(r, carry):
        _row_copy(x_hbm, o_hbm, sem, src_ref[base + r], base + r, 1).start()
        return carry

    lax.fori_loop(0, rows, issue, 0, unroll=8)
    _row_copy(x_hbm, o_hbm, sem, 0, base, rows).wait()


def gather_rows(x, src, rows_pref=512):
    p = src.shape[0]
    d = x.shape[1]
    rows = _tile(p, rows_pref, SUBLANES)
    return pl.pallas_call(
        functools.partial(_gather_body, rows=rows),
        grid_spec=pltpu.PrefetchScalarGridSpec(
            num_scalar_prefetch=1,
            grid=(p // rows,),
            in_specs=[pl.BlockSpec(memory_space=pl.ANY)],
            out_specs=pl.BlockSpec(memory_space=pl.ANY),
            scratch_shapes=[pltpu.SemaphoreType.DMA(())],
        ),
        out_shape=jax.ShapeDtypeStruct((p, d), x.dtype),
        compiler_params=_params("arbitrary"),
        name="moe_gather",
    )(src, x)


def _gmm_body(te_ref, tv_ref, xs_ref, gf_ref, wgu_ref, wd_ref, y_ref, *, d_expert):
    del te_ref
    valid = tv_ref[pl.program_id(0)] > 0

    @pl.when(valid)
    def _():
        hn = _rms(xs_ref[...], gf_ref[...]).astype(BF16)
        au = _dot(hn, wgu_ref[0])
        a = au[:, :d_expert]
        act = (a * _sigmoid(a) * au[:, d_expert:]).astype(BF16)
        y_ref[...] = _dot(act, wd_ref[0])

    @pl.when(jnp.logical_not(valid))
    def _():
        y_ref[...] = jnp.zeros_like(y_ref)


def grouped_mlp(xs, gf, wgu, wd, tile_expert, tile_valid, tm):
    p, d = xs.shape
    d_expert = wd.shape[1]
    return pl.pallas_call(
        functools.partial(_gmm_body, d_expert=d_expert),
        grid_spec=pltpu.PrefetchScalarGridSpec(
            num_scalar_prefetch=2,
            grid=(p // tm,),
            in_specs=[
                pl.BlockSpec((tm, d), lambda i, te, tv: (i, 0)),
                pl.BlockSpec((1, d), lambda i, te, tv: (0, 0)),
                pl.BlockSpec((1, d, 2 * d_expert), lambda i, te, tv: (te[i], 0, 0)),
                pl.BlockSpec((1, d_expert, d), lambda i, te, tv: (te[i], 0, 0)),
            ],
            out_specs=pl.BlockSpec((tm, d), lambda i, te, tv: (i, 0)),
        ),
        out_shape=jax.ShapeDtypeStruct((p, d), F32),
        compiler_params=_params("arbitrary"),
        name="moe_grouped_mlp",
    )(tile_expert, tile_valid, xs, gf, wgu, wd)


def _combine_body(pos_ref, x1_ref, route_ref, gn_ref, y_hbm, o_ref, buf, sem, *, tm, final_norm):
    i = pl.program_id(0)
    n = pl.num_programs(0)
    n_copies = TOP_K * tm

    def fetch(tile, slot):
        def issue(r, carry):
            for k in range(TOP_K):
                src_row = pos_ref[(tile * tm + r) * TOP_K + k]
                pltpu.make_async_copy(y_hbm.at[pl.ds(src_row, 1)], buf.at[slot, pl.ds(k * tm + r, 1)],
                                      sem.at[slot]).start()
            return carry

        lax.fori_loop(0, tm, issue, 0, unroll=4)

    @pl.when(i == 0)
    def _():
        fetch(0, 0)

    @pl.when(i + 1 < n)
    def _():
        fetch(i + 1, (i + 1) % 2)

    slot = i % 2
    pltpu.make_async_copy(y_hbm.at[pl.ds(0, n_copies)], buf.at[slot], sem.at[slot]).wait()
    rt = route_ref[...]
    x2 = x1_ref[...] + (rt[:, 2:3] * buf[slot, 0:tm, :] + rt[:, 3:4] * buf[slot, tm:2 * tm, :])
    if final_norm:
        x2 = _rms(x2, gn_ref[...])
    o_ref[...] = x2


def combine(x1, route, y, pos, gn, final_norm, tm_pref=256):
    t, d = x1.shape
    tm = _tile(t, tm_pref, SUBLANES)
    return pl.pallas_call(
        functools.partial(_combine_body, tm=tm, final_norm=final_norm),
        grid_spec=pltpu.PrefetchScalarGridSpec(
            num_scalar_prefetch=1,
            grid=(t // tm,),
            in_specs=[
                pl.BlockSpec((tm, d), lambda i, pos: (i, 0)),
                pl.BlockSpec((tm, LANES), lambda i, pos: (i, 0)),
                pl.BlockSpec((1, d), lambda i, pos: (0, 0)),
                pl.BlockSpec(memory_space=pl.ANY),
            ],
            out_specs=pl.BlockSpec((tm, d), lambda i, pos: (i, 0)),
            scratch_shapes=[pltpu.VMEM((2, TOP_K * tm, d), F32), pltpu.SemaphoreType.DMA((2,))],
        ),
        out_shape=jax.ShapeDtypeStruct((t, d), F32),
        compiler_params=_params("arbitrary"),
        name="moe_combine",
    )(pos, x1, route, gn, y)


def _dispatch_plan(route, n_experts, tm):
    t = route.shape[0]
    n_assign = TOP_K * t
    eid = route[:, :TOP_K].astype(jnp.int32).reshape(n_assign)
    onehot = (eid[:, None] == jnp.arange(n_experts, dtype=jnp.int32)[None, :]).astype(jnp.int32)
    csum = jnp.cumsum(onehot, axis=0)
    rank = jnp.sum((csum - onehot) * onehot, axis=1)
    counts = csum[-1]
    padded = ((counts + tm - 1) // tm) * tm
    ends = jnp.cumsum(padded)
    starts = ends - padded
    pos = (starts[eid] + rank).astype(jnp.int32)
    n_tiles = (n_assign + n_experts * (tm - 1)) // tm
    src = jnp.zeros((n_tiles * tm,), jnp.int32).at[pos].set(jnp.arange(n_assign, dtype=jnp.int32) // TOP_K)
    tile_start = jnp.arange(n_tiles, dtype=jnp.int32) * tm
    tile_expert = jnp.minimum(jnp.searchsorted(ends, tile_start, side="right"), n_experts - 1).astype(jnp.int32)
    tile_valid = (tile_start < ends[-1]).astype(jnp.int32)
    return pos, src, tile_expert, tile_valid


def hier_moe(x1, route, gf, wgu, wd, gn, final_norm, tm=256):
    n_experts = wgu.shape[0]
    pos, src, tile_expert, tile_valid = _dispatch_plan(route, n_experts, tm)
    xs = gather_rows(x1, src)
    y = grouped_mlp(xs, gf, wgu, wd, tile_expert, tile_valid, tm)
    return combine(x1, route, y, pos, gn, final_norm)


def _rope_tables(pos):
    half = ROPE_DIM // 2
    inv_freq = ROPE_THETA ** (-jnp.arange(half, dtype=F32) / half)
    ang = pos.astype(F32)[:, None] * inv_freq[None, :]
    z = jnp.zeros((pos.shape[0], LANES - ROPE_DIM), F32)
    cos, sin = jnp.cos(ang), jnp.sin(ang)
    return jnp.concatenate([cos, cos, z], axis=1), jnp.concatenate([sin, sin, z], axis=1)


def _swap_halves(w):
    half = ROPE_DIM // 2
    return jnp.concatenate([-w[..., half:], w[..., :half]], axis=-1)


def kernel(x_prompt, x_sample, mem_prompt, cache_mla_ckv, cache_mla_krope, cache_mem_k, cache_mem_v, state_rglru_h, state_rglru_conv, norm_mix, w_in, conv_w, conv_b, lru_wa, lru_ba, lru_wx, lru_bx, lru_lambda, mla_q_norm, mla_w_uq, mla_kv_norm, mla_w_uk, mla_w_uv, mem_norm, mem_w_k, mem_w_v, w_branch_rec, w_branch_mla, w_branch_mem, w_out, norm_ffn, router_group_w, router_group_b, router_expert_w, router_expert_b, moe_w_gate, moe_w_up, moe_w_down, norm_final):
    bp, sp, d = x_prompt.shape
    bs, ss, _ = x_sample.shape
    depth = w_in.shape[0]
    past = cache_mla_ckv.shape[2]
    m_tok = mem_prompt.shape[1]
    d_rnn = conv_w.shape[2]
    q_lora = mla_q_norm.shape[1]
    kv_lora = mla_kv_norm.shape[1]
    n_heads = mla_w_uq.shape[2]
    mem_heads = cache_mem_k.shape[3]
    mem_q = mem_heads * cache_mem_k.shape[4]
    n_experts = moe_w_gate.shape[1]
    tp, tsm = bp * sp, bs * ss
    scale = (NOPE_DIM + ROPE_DIM) ** -0.5

    x = jnp.concatenate([x_prompt.reshape(tp, d), x_sample.reshape(tsm, d)], axis=0)
    pos_all = jnp.concatenate([jnp.tile(jnp.arange(sp), bp), jnp.tile(past + jnp.arange(ss), bs)])
    cos, sin = _rope_tables(pos_all)
    mem_flat = mem_prompt.reshape(bp * m_tok, d)
    cache_mem_k = cache_mem_k.reshape(depth, bs, m_tok, mem_q)
    cache_mem_v = cache_mem_v.reshape(depth, bs, m_tok, mem_q)
    conv_zero = jnp.zeros((bp, CONV_W - 1, d_rnn), F32)
    h_zero = jnp.zeros((bp, 1, d_rnn), F32)
    row2 = lambda v: v.reshape(1, -1)

    outs = {k: [] for k in ("p_ckv", "p_kr", "p_mk", "p_mv", "p_h", "p_conv", "s_ckv", "s_kr", "s_h", "s_conv")}
    for l in range(depth):
        o = 0
        segs = []
        for wdt in (d_rnn, d_rnn, q_lora, kv_lora, ROPE_DIM, mem_q, 3 * d):
            segs.append(w_in[l][:, o:o + wdt])
            o += wdt
        w_ux, w_uy, w_cq, w_ckv, w_kr, w_qm, w_g = segs
        w_xym = jnp.concatenate([w_ux, w_uy, w_qm], axis=1).astype(BF16)
        w_cq = w_cq.astype(BF16)
        w_kv = jnp.concatenate([w_ckv, w_kr, _swap_halves(w_kr)], axis=1).astype(BF16)
        w_g = w_g.astype(BF16)
        uq = mla_w_uq[l]
        uq_rope = uq[..., NOPE_DIM:]
        w_uq = jnp.concatenate([uq, _swap_halves(uq_rope)], axis=-1).reshape(q_lora, n_heads * HEAD_PAD).astype(BF16)
        w_uk = mla_w_uk[l].reshape(kv_lora, n_heads * NOPE_DIM).astype(BF16)
        w_uv = mla_w_uv[l].reshape(kv_lora, n_heads * V_DIM).astype(BF16)
        w_ukt = jnp.transpose(mla_w_uk[l], (1, 2, 0)).astype(BF16)
        w_uvh = jnp.transpose(mla_w_uv[l], (1, 0, 2)).astype(BF16)
        w_memkv = jnp.concatenate([mem_w_k[l], mem_w_v[l]], axis=1).astype(BF16)
        w_r32 = jnp.concatenate([router_group_w[l], router_expert_w[l],
                                 jnp.zeros((d, LANES - N_GROUPS - n_experts), F32)], axis=1)
        w_r = w_r32.astype(BF16)
        w_r_lo = (w_r32 - w_r.astype(F32)).astype(BF16)
        b_r = jnp.concatenate([router_group_b[l], router_expert_b[l],
                               jnp.zeros((LANES - N_GROUPS - n_experts,), F32)]).reshape(1, LANES)
        w_gu = jnp.concatenate([moe_w_gate[l], moe_w_up[l]], axis=-1).astype(BF16)
        w_dn = moe_w_down[l].astype(BF16)

        mkv, _ = norm_matmul(mem_flat, row2(mem_norm[l]), w_memkv, F32)
        mk = mkv[:, :mem_q].reshape(bp, m_tok, mem_q)
        mv = mkv[:, mem_q:].reshape(bp, m_tok, mem_q)
        outs["p_mk"].append(mk.reshape(bp, m_tok, mem_heads, -1))
        outs["p_mv"].append(mv.reshape(bp, m_tok, mem_heads, -1))

        z, xn = norm_matmul(x, row2(norm_mix[l]), w_xym, BF16)

        lru = (conv_w[l], row2(conv_b[l]), lru_wa[l].astype(BF16), lru_wx[l].astype(BF16),
               row2(lru_ba[l]), row2(lru_bx[l]), row2(lru_lambda[l]))
        rec_p, hl_p, cb_p = rglru(z, 0, bp, sp, conv_zero, h_zero, *lru, d_rnn)
        rec_s, hl_s, cb_s = rglru(z, tp, bs, ss, state_rglru_conv[l], state_rglru_h[l].reshape(bs, 1, d_rnn),
                                  *lru, d_rnn)
        outs["p_h"].append(hl_p.reshape(bp, d_rnn)); outs["p_conv"].append(cb_p)
        outs["s_h"].append(hl_s.reshape(bs, d_rnn)); outs["s_conv"].append(cb_s)

        q = q_proj(xn, w_cq, row2(mla_q_norm[l]), w_uq, cos, sin, n_heads, scale)
        ckv, kr, k, v = kv_proj(xn, w_kv, row2(mla_kv_norm[l]), w_uk, w_uv, cos, sin, n_heads)
        outs["p_ckv"].append(ckv[:tp].reshape(bp, sp, kv_lora)); outs["p_kr"].append(kr[:tp].reshape(bp, sp, ROPE_DIM))
        outs["s_ckv"].append(ckv[tp:].reshape(bs, ss, kv_lora)); outs["s_kr"].append(kr[tp:].reshape(bs, ss, ROPE_DIM))
        mla_p = attn_prompt(q, k, v, bp, sp)
        mla_s = attn_cached(q, ckv, kr, cache_mla_ckv, cache_mla_krope, l, w_ukt, w_uvh, tp, bs, ss)

        col_blk = (2 * d_rnn) // mem_q
        mem_p = mem_attn(z, col_blk, mk[None], mv[None], 0, 0, bp, sp, mem_heads)
        mem_s = mem_attn(z, col_blk, cache_mem_k, cache_mem_v, l, tp, bs, ss, mem_heads)

        merged = merge_branches(xn, (rec_p, mla_p, mem_p), (rec_s, mla_s, mem_s), w_g,
                                w_branch_rec[l].astype(BF16),
                                w_branch_mla[l].astype(BF16), w_branch_mem[l].astype(BF16))
        x1, route = out_router(x, merged, w_out[l].astype(BF16), row2(norm_ffn[l]), w_r, w_r_lo, b_r, n_experts)
        x = hier_moe(x1, route, row2(norm_ffn[l]), w_gu, w_dn, row2(norm_final), l == depth - 1)

    st = lambda name: jnp.stack(outs[name])
    return (x[:tp].reshape(bp, sp, d), x[tp:].reshape(bs, ss, d),
            st("p_ckv"), st("p_kr"), st("p_mk"), st("p_mv"), st("p_h"), st("p_conv"),
            st("s_ckv"), st("s_kr"), st("s_h"), st("s_conv"))
```

```python
import functools

import jax
import jax.numpy as jnp
from jax import lax
from jax.experimental import pallas as pl
from jax.experimental.pallas import tpu as pltpu

F32 = jnp.float32
BF16 = jnp.bfloat16

EPS = 1e-6
CHUNK = 64
LRU_C = 8.0
ROPE_THETA = 10000.0
CONV_W = 4
NOPE_DIM = 128
ROPE_DIM = 64
V_DIM = 128
N_GROUPS = 4
TOP_K = 2

LANES = 128
SUBLANES = 8
HEAD_PAD = 2 * LANES
V7X_VMEM_BYTES = 64 * 1024 * 1024
VMEM_LIMIT = (V7X_VMEM_BYTES * 7) // 8
NEG_BIG = -1e30


def _params(*sem):
    return pltpu.CompilerParams(dimension_semantics=sem, vmem_limit_bytes=VMEM_LIMIT)


def _tile(n, pref, mult=SUBLANES):
    t = min(pref, n)
    t -= t % mult
    while n % t:
        t -= mult
    return t


def _rms(x, g):
    y = x * lax.rsqrt(jnp.mean(x * x, axis=-1, keepdims=True) + EPS)
    return y * g


def _gelu(x):
    return 0.5 * x * (1.0 + jnp.tanh(0.7978845608028654 * (x + 0.044715 * (x * x * x))))


def _sigmoid(x):
    return 1.0 / (1.0 + jnp.exp(-x))


def _dot(a, b):
    return jnp.dot(a, b, preferred_element_type=F32)


def _dot_t(a, b):
    return lax.dot_general(a, b, (((1,), (1,)), ((), ())), preferred_element_type=F32)


def _norm_matmul_body(x_ref, g_ref, w_ref, z_ref, xn_ref, xn_scr):
    @pl.when(pl.program_id(1) == 0)
    def _():
        xn = _rms(x_ref[...], g_ref[...]).astype(BF16)
        xn_scr[...] = xn
        xn_ref[...] = xn

    z_ref[...] = _dot(xn_scr[...], w_ref[...]).astype(z_ref.dtype)


def norm_matmul(x, g, w, out_dtype, tm_pref=512, tn_pref=1024):
    t, d = x.shape
    n = w.shape[1]
    tm, tn = _tile(t, tm_pref, 16), _tile(n, tn_pref, LANES)
    return pl.pallas_call(
        _norm_matmul_body,
        grid=(t // tm, n // tn),
        in_specs=[
            pl.BlockSpec((tm, d), lambda i, j: (i, 0)),
            pl.BlockSpec((1, d), lambda i, j: (0, 0)),
            pl.BlockSpec((d, tn), lambda i, j: (0, j)),
        ],
        out_specs=[
            pl.BlockSpec((tm, tn), lambda i, j: (i, j)),
            pl.BlockSpec((tm, d), lambda i, j: (i, 0)),
        ],
        out_shape=[jax.ShapeDtypeStruct((t, n), out_dtype), jax.ShapeDtypeStruct((t, d), BF16)],
        scratch_shapes=[pltpu.VMEM((tm, d), BF16)],
        compiler_params=_params("arbitrary", "arbitrary"),
        name="norm_matmul",
    )(x, g, w)


def _rope_mix(r, c, s):
    return r * c + pltpu.roll(r, ROPE_DIM, 1) * s


def _q_proj_body(xn_ref, wq_ref, gq_ref, wuq_ref, cos_ref, sin_ref, q_ref, *, n_heads, scale):
    cq = _dot(xn_ref[...], wq_ref[...])
    cqn = _rms(cq, gq_ref[...]).astype(BF16)
    c = cos_ref[...] * scale
    s = sin_ref[...] * scale
    for h in range(n_heads):
        qh = _dot(cqn, wuq_ref[:, h * HEAD_PAD:(h + 1) * HEAD_PAD])
        q_ref[h, :, :NOPE_DIM] = (qh[:, :NOPE_DIM] * scale).astype(BF16)
        q_ref[h, :, NOPE_DIM:] = _rope_mix(qh[:, NOPE_DIM:], c, s).astype(BF16)


def q_proj(xn, wq, gq, wuq, cos, sin, n_heads, scale, tm_pref=512):
    t, d = xn.shape
    ql = wq.shape[1]
    tm = _tile(t, tm_pref, 16)
    return pl.pallas_call(
        functools.partial(_q_proj_body, n_heads=n_heads, scale=scale),
        grid=(t // tm,),
        in_specs=[
            pl.BlockSpec((tm, d), lambda i: (i, 0)),
            pl.BlockSpec((d, ql), lambda i: (0, 0)),
            pl.BlockSpec((1, ql), lambda i: (0, 0)),
            pl.BlockSpec((ql, n_heads * HEAD_PAD), lambda i: (0, 0)),
            pl.BlockSpec((tm, LANES), lambda i: (i, 0)),
            pl.BlockSpec((tm, LANES), lambda i: (i, 0)),
        ],
        out_specs=pl.BlockSpec((n_heads, tm, HEAD_PAD), lambda i: (0, i, 0)),
        out_shape=jax.ShapeDtypeStruct((n_heads, t, HEAD_PAD), BF16),
        compiler_params=_params("arbitrary"),
        name="q_proj",
    )(xn, wq, gq, wuq, cos, sin)


def _kv_proj_body(xn_ref, wkv_ref, gkv_ref, wuk_ref, wuv_ref, cos_ref, sin_ref,
                  ckv_ref, kr_ref, k_ref, v_ref, *, n_heads, kv_lora):
    ckr = _dot(xn_ref[...], wkv_ref[...])
    ckv = _rms(ckr[:, :kv_lora], gkv_ref[...])
    ckv_ref[...] = ckv
    rr = _rope_mix(ckr[:, kv_lora:], cos_ref[...], sin_ref[...])
    kr_ref[...] = rr[:, :ROPE_DIM]
    ckv_b = ckv.astype(BF16)
    rr_b = rr.astype(BF16)
    kn = _dot(ckv_b, wuk_ref[...])
    vv = _dot(ckv_b, wuv_ref[...])
    for h in range(n_heads):
        k_ref[h, :, :NOPE_DIM] = kn[:, h * NOPE_DIM:(h + 1) * NOPE_DIM].astype(BF16)
        k_ref[h, :, NOPE_DIM:] = rr_b
        v_ref[h] = vv[:, h * V_DIM:(h + 1) * V_DIM].astype(BF16)


def kv_proj(xn, wkv, gkv, wuk, wuv, cos, sin, n_heads, tm_pref=512):
    t, d = xn.shape
    kv_lora = gkv.shape[1]
    tm = _tile(t, tm_pref, 16)
    return pl.pallas_call(
        functools.partial(_kv_proj_body, n_heads=n_heads, kv_lora=kv_lora),
        grid=(t // tm,),
        in_specs=[
            pl.BlockSpec((tm, d), lambda i: (i, 0)),
            pl.BlockSpec((d, kv_lora + LANES), lambda i: (0, 0)),
            pl.BlockSpec((1, kv_lora), lambda i: (0, 0)),
            pl.BlockSpec((kv_lora, n_heads * NOPE_DIM), lambda i: (0, 0)),
            pl.BlockSpec((kv_lora, n_heads * V_DIM), lambda i: (0, 0)),
            pl.BlockSpec((tm, LANES), lambda i: (i, 0)),
            pl.BlockSpec((tm, LANES), lambda i: (i, 0)),
        ],
        out_specs=[
            pl.BlockSpec((tm, kv_lora), lambda i: (i, 0)),
            pl.BlockSpec((tm, ROPE_DIM), lambda i: (i, 0)),
            pl.BlockSpec((n_heads, tm, HEAD_PAD), lambda i: (0, i, 0)),
            pl.BlockSpec((n_heads, tm, V_DIM), lambda i: (0, i, 0)),
        ],
        out_shape=[
            jax.ShapeDtypeStruct((t, kv_lora), F32),
            jax.ShapeDtypeStruct((t, ROPE_DIM), F32),
            jax.ShapeDtypeStruct((n_heads, t, HEAD_PAD), BF16),
            jax.ShapeDtypeStruct((n_heads, t, V_DIM), BF16),
        ],
        compiler_params=_params("arbitrary"),
        name="kv_proj",
    )(xn, wkv, gkv, wuk, wuv, cos, sin)


def _rglru_body(ux_ref, uy_ref, conv0_ref, h0_ref, cw_ref, cb_ref, wa_ref, wx_ref, ba_ref, bx_ref,
                lam_ref, o_ref, hlast_ref, nconv_ref, ubuf, a_scr, b_scr, hcar, *, ts, n_blocks):
    hist = SUBLANES
    n_col = a_scr.shape[0]
    d = n_col * LANES
    col = lambda c: slice(c * LANES, (c + 1) * LANES)

    @pl.when(pl.program_id(1) == 0)
    def _():
        ubuf[hist - (CONV_W - 1):hist, :] = conv0_ref[0]
        hcar[...] = h0_ref[0]

    ubuf[hist:hist + ts, :] = ux_ref[...].astype(F32)
    cw = cw_ref[...]
    xc = cb_ref[...]
    for k in range(CONV_W):
        off = hist - (CONV_W - 1) + k
        xc = xc + ubuf[off:off + ts, :] * cw[k:k + 1, :]
    tail = ubuf[hist + ts - (CONV_W - 1):hist + ts, :]
    nconv_ref[0] = tail
    ubuf[hist - (CONV_W - 1):hist, :] = tail

    lam = lam_ref[...]
    sp = jnp.maximum(-lam, 0.0) + jnp.log(1.0 + jnp.exp(-jnp.abs(lam)))
    for n in range(n_blocks):
        sl = col(n)
        xcn = xc[:, sl]
        xb = xcn.astype(BF16)
        r = _sigmoid(_dot(xb, wa_ref[n]) + ba_ref[:, sl])
        i = _sigmoid(_dot(xb, wx_ref[n]) + bx_ref[:, sl])
        a = jnp.exp((-LRU_C) * r * sp[:, sl])
        a_scr[n] = a
        b_scr[n] = jnp.sqrt(1.0 - a * a) * (i * xcn)

    seg = ts // SUBLANES

    def step(t, carry):
        h, p = carry
        rows = pl.ds(t, SUBLANES, stride=seg)
        hs, ps = [], []
        for c in range(n_col):
            a = a_scr[c, rows, :]
            hc = a * h[:, col(c)] + b_scr[c, rows, :]
            pc = a * p[:, col(c)]
            b_scr[c, rows, :] = hc
            a_scr[c, rows, :] = pc
            hs.append(hc)
            ps.append(pc)
        return jnp.concatenate(hs, axis=1), jnp.concatenate(ps, axis=1)

    h, p = lax.fori_loop(0, seg, step, (jnp.zeros((SUBLANES, d), F32), jnp.ones((SUBLANES, d), F32)))
    carry_in = hcar[...]
    for s in range(SUBLANES):
        rows = slice(s * seg, (s + 1) * seg)
        for c in range(n_col):
            b_scr[c, rows, :] = b_scr[c, rows, :] + a_scr[c, rows, :] * carry_in[:, col(c)]
        carry_in = h[s:s + 1, :] + p[s:s + 1, :] * carry_in
    hcar[...] = carry_in
    hlast_ref[0] = carry_in
    for c in range(n_col):
        o_ref[:, col(c)] = (b_scr[c] * _gelu(uy_ref[:, col(c)].astype(F32))).astype(o_ref.dtype)


def rglru(z, row0, n_batch, seq, conv0, h0, cw, cb, wa, wx, ba, bx, lam, d_rnn, ts_pref=256):
    ts = _tile(seq, ts_pref, 16)
    nst = seq // ts
    blk0 = row0 // ts
    n_blocks = wa.shape[0]
    assert d_rnn == n_blocks * LANES and ts % SUBLANES == 0
    vec = pl.BlockSpec((1, d_rnn), lambda b, s: (0, 0))
    in_specs = [
        pl.BlockSpec((ts, d_rnn), lambda b, s: (blk0 + b * nst + s, 0)),
        pl.BlockSpec((ts, d_rnn), lambda b, s: (blk0 + b * nst + s, 1)),
        pl.BlockSpec((1, CONV_W - 1, d_rnn), lambda b, s: (b, 0, 0)),
        pl.BlockSpec((1, 1, d_rnn), lambda b, s: (b, 0, 0)),
        pl.BlockSpec((CONV_W, d_rnn), lambda b, s: (0, 0)),
        vec,
        pl.BlockSpec(wa.shape, lambda b, s: (0, 0, 0)),
        pl.BlockSpec(wx.shape, lambda b, s: (0, 0, 0)),
        vec, vec, vec,
    ]
    return pl.pallas_call(
        functools.partial(_rglru_body, ts=ts, n_blocks=n_blocks),
        grid=(n_batch, nst),
        in_specs=in_specs,
        out_specs=[
            pl.BlockSpec((ts, d_rnn), lambda b, s: (b * nst + s, 0)),
            pl.BlockSpec((1, 1, d_rnn), lambda b, s: (b, 0, 0)),
            pl.BlockSpec((1, CONV_W - 1, d_rnn), lambda b, s: (b, 0, 0)),
        ],
        out_shape=[
            jax.ShapeDtypeStruct((n_batch * seq, d_rnn), BF16),
            jax.ShapeDtypeStruct((n_batch, 1, d_rnn), F32),
            jax.ShapeDtypeStruct((n_batch, CONV_W - 1, d_rnn), F32),
        ],
        scratch_shapes=[
            pltpu.VMEM((SUBLANES + ts, d_rnn), F32),
            pltpu.VMEM((d_rnn // LANES, ts, LANES), F32),
            pltpu.VMEM((d_rnn // LANES, ts, LANES), F32),
            pltpu.VMEM((1, d_rnn), F32),
        ],
        compiler_params=_params("arbitrary", "arbitrary"),
        name="rglru",
    )(z, z, conv0, h0, cw, cb, wa, wx, ba, bx, lam)


def _softmax_step(s, v, m, l, acc):
    m_new = jnp.maximum(m, jnp.max(s, axis=-1, keepdims=True))
    alpha = jnp.exp(m - m_new)
    p = jnp.exp(s - m_new)
    l = alpha * l + jnp.sum(p, axis=-1, keepdims=True)
    acc = alpha * acc + _dot(p.astype(BF16), v)
    return m_new, l, acc


def _attn_prompt_body(q_ref, k_ref, v_ref, o_ref, *, tq, hpb):
    qi = pl.program_id(2)
    qs = [q_ref[h] for h in range(hpb)]

    def kv(h, j):
        off = pl.multiple_of(j * tq, tq)
        return k_ref[h, pl.ds(off, tq), :], v_ref[h, pl.ds(off, tq), :]

    def body(j, carry):
        out = []
        for h in range(hpb):
            k, v = kv(h, j)
            out.append(_softmax_step(_dot_t(qs[h], k), v, *carry[h]))
        return tuple(out)

    init = (jnp.full((tq, 1), NEG_BIG, F32), jnp.zeros((tq, 1), F32), jnp.zeros((tq, V_DIM), F32))
    carry = lax.fori_loop(0, qi, body, (init,) * hpb)
    shift = CHUNK.bit_length() - 1
    row_chunk = lax.broadcasted_iota(jnp.int32, (tq, tq), 0) >> shift
    col_chunk = lax.broadcasted_iota(jnp.int32, (tq, tq), 1) >> shift
    visible = col_chunk <= row_chunk
    for h in range(hpb):
        k, v = kv(h, qi)
        s = jnp.where(visible, _dot_t(qs[h], k), NEG_BIG)
        m, l, acc = _softmax_step(s, v, *carry[h])
        o_ref[:, h * V_DIM:(h + 1) * V_DIM] = (acc / l).astype(o_ref.dtype)


def attn_prompt(q, k, v, n_batch, seq, tq_pref=512, hpb=2):
    n_heads = q.shape[0]
    assert n_heads % hpb == 0
    tq = _tile(seq, tq_pref, CHUNK)
    nq = seq // tq
    return pl.pallas_call(
        functools.partial(_attn_prompt_body, tq=tq, hpb=hpb),
        grid=(n_batch, n_heads // hpb, nq),
        in_specs=[
            pl.BlockSpec((hpb, tq, HEAD_PAD), lambda b, h, i: (h, b * nq + i, 0)),
            pl.BlockSpec((hpb, seq, HEAD_PAD), lambda b, h, i: (h, b, 0)),
            pl.BlockSpec((hpb, seq, V_DIM), lambda b, h, i: (h, b, 0)),
        ],
        out_specs=pl.BlockSpec((tq, hpb * V_DIM), lambda b, h, i: (b * nq + i, h)),
        out_shape=jax.ShapeDtypeStruct((n_batch * seq, n_heads * V_DIM), BF16),
        compiler_params=_params("arbitrary", "arbitrary", "arbitrary"),
        name="attn_prompt",
    )(q, k, v)


def _attn_cached_body(q_ref, ckvn_ref, krn_ref, cckv_ref, ckr_ref, wukt_ref, wuv_ref, o_ref,
                      ql_scr, qr_scr, *, n_heads, sq, tk, past):
    for h in range(n_heads):
        qh = q_ref[h]
        ql_scr[h * sq:(h + 1) * sq, :] = _dot(qh[:, :NOPE_DIM], wukt_ref[h]).astype(BF16)
        qr_scr[h * sq:(h + 1) * sq, :] = qh[:, NOPE_DIM:NOPE_DIM + ROPE_DIM]
    ql = ql_scr[...]
    qr = qr_scr[...]
    rows = n_heads * sq
    kv_lora = ql.shape[1]

    def body(j, carry):
        off = pl.multiple_of(j * tk, tk)
        kc = cckv_ref[0, 0, pl.ds(off, tk), :].astype(BF16)
        kr = ckr_ref[0, 0, pl.ds(off, tk), :].astype(BF16)
        return _softmax_step(_dot_t(ql, kc) + _dot_t(qr, kr), kc, *carry)

    init = (jnp.full((rows, 1), NEG_BIG, F32), jnp.zeros((rows, 1), F32), jnp.zeros((rows, kv_lora), F32))
    carry = lax.fori_loop(0, past // tk, body, init)
    kc = ckvn_ref[...].astype(BF16)
    kr = krn_ref[...].astype(BF16)
    m, l, acc = _softmax_step(_dot_t(ql, kc) + _dot_t(qr, kr), kc, *carry)
    o_lat = (acc / l).astype(BF16)
    for h in range(n_heads):
        o_ref[:, h * V_DIM:(h + 1) * V_DIM] = _dot(o_lat[h * sq:(h + 1) * sq, :], wuv_ref[h]).astype(o_ref.dtype)


def attn_cached(q, ckv, kr, cache_ckv, cache_kr, layer, wukt, wuv, row0, n_batch, sq, tk_pref=512):
    n_heads = q.shape[0]
    kv_lora = ckv.shape[1]
    past = cache_ckv.shape[2]
    tk = _tile(past, tk_pref, 16)
    blk0 = row0 // sq
    return pl.pallas_call(
        functools.partial(_attn_cached_body, n_heads=n_heads, sq=sq, tk=tk, past=past),
        grid=(n_batch,),
        in_specs=[
            pl.BlockSpec((n_heads, sq, HEAD_PAD), lambda b: (0, blk0 + b, 0)),
            pl.BlockSpec((sq, kv_lora), lambda b: (blk0 + b, 0)),
            pl.BlockSpec((sq, ROPE_DIM), lambda b: (blk0 + b, 0)),
            pl.BlockSpec((1, 1, past, kv_lora), lambda b: (layer, b, 0, 0)),
            pl.BlockSpec((1, 1, past, ROPE_DIM), lambda b: (layer, b, 0, 0)),
            pl.BlockSpec(wukt.shape, lambda b: (0, 0, 0)),
            pl.BlockSpec(wuv.shape, lambda b: (0, 0, 0)),
        ],
        out_specs=pl.BlockSpec((sq, n_heads * V_DIM), lambda b: (b, 0)),
        out_shape=jax.ShapeDtypeStruct((n_batch * sq, n_heads * V_DIM), BF16),
        scratch_shapes=[pltpu.VMEM((n_heads * sq, kv_lora), BF16), pltpu.VMEM((n_heads * sq, ROPE_DIM), BF16)],
        compiler_params=_params("arbitrary"),
        name="attn_cached",
    )(q, ckv, kr, cache_ckv, cache_kr, wukt, wuv)


def _mem_attn_body(q_ref, k_ref, v_ref, o_ref, *, n_heads):
    hd = q_ref.shape[1] // n_heads
    scale = hd ** -0.5
    for h in range(n_heads):
        sl = slice(h * hd, (h + 1) * hd)
        s = _dot_t(q_ref[:, sl], k_ref[0, 0, :, sl].astype(BF16)) * scale
        e = jnp.exp(s - jnp.max(s, axis=-1, keepdims=True))
        o = _dot(e.astype(BF16), v_ref[0, 0, :, sl].astype(BF16)) / jnp.sum(e, axis=-1, keepdims=True)
        o_ref[:, sl] = o.astype(o_ref.dtype)


def mem_attn(z, col_blk, mem_k, mem_v, layer, row0, n_batch, seq, n_heads, tm_pref=512):
    _, _, m_tok, dq = mem_k.shape
    tm = _tile(seq, tm_pref, 16)
    nst = seq // tm
    blk0 = row0 // tm
    return pl.pallas_call(
        functools.partial(_mem_attn_body, n_heads=n_heads),
        grid=(n_batch, nst),
        in_specs=[
            pl.BlockSpec((tm, dq), lambda b, s: (blk0 + b * nst + s, col_blk)),
            pl.BlockSpec((1, 1, m_tok, dq), lambda b, s: (layer, b, 0, 0)),
            pl.BlockSpec((1, 1, m_tok, dq), lambda b, s: (layer, b, 0, 0)),
        ],
        out_specs=pl.BlockSpec((tm, dq), lambda b, s: (b * nst + s, 0)),
        out_shape=jax.ShapeDtypeStruct((n_batch * seq, dq), BF16),
        compiler_params=_params("arbitrary", "arbitrary"),
        name="mem_attn",
    )(z, mem_k, mem_v)


def _merge_body(xn_ref, p0_ref, p1_ref, p2_ref, s0_ref, s1_ref, s2_ref,
                g0_ref, g1_ref, g2_ref, w0_ref, w1_ref, w2_ref, o_ref, *, n_first):
    def emit(a0_ref, a1_ref, a2_ref):
        xn = xn_ref[...]
        acc = _sigmoid(_dot(xn, g0_ref[...])) * _dot(a0_ref[...], w0_ref[...])
        acc = acc + _sigmoid(_dot(xn, g1_ref[...])) * _dot(a1_ref[...], w1_ref[...])
        acc = acc + _sigmoid(_dot(xn, g2_ref[...])) * _dot(a2_ref[...], w2_ref[...])
        o_ref[...] = acc.astype(o_ref.dtype)

    i = pl.program_id(1)

    @pl.when(i < n_first)
    def _():
        emit(p0_ref, p1_ref, p2_ref)

    @pl.when(i >= n_first)
    def _():
        emit(s0_ref, s1_ref, s2_ref)


def merge_branches(xn, first, second, wg, w0, w1, w2, tm_pref=256, tn_pref=512):
    t, d = xn.shape
    t1, t2 = first[0].shape[0], second[0].shape[0]
    tm, tn = _tile(min(t1, t2), tm_pref, 16), _tile(d, tn_pref, LANES)
    assert t1 % tm == 0 and t2 % tm == 0 and t1 + t2 == t
    n1 = t1 // tm
    nj = d // tn
    act = pl.BlockSpec((tm, d), lambda j, i: (i, 0))
    act1 = pl.BlockSpec((tm, d), lambda j, i: (jnp.minimum(i, n1 - 1), 0))
    act2 = pl.BlockSpec((tm, d), lambda j, i: (jnp.maximum(i - n1, 0), 0))
    wsp = pl.BlockSpec((d, tn), lambda j, i: (0, j))
    return pl.pallas_call(
        functools.partial(_merge_body, n_first=n1),
        grid=(nj, t // tm),
        in_specs=[act, act1, act1, act1, act2, act2, act2,
                  pl.BlockSpec((d, tn), lambda j, i: (0, j)),
                  pl.BlockSpec((d, tn), lambda j, i: (0, nj + j)),
                  pl.BlockSpec((d, tn), lambda j, i: (0, 2 * nj + j)),
                  wsp, wsp, wsp],
        out_specs=pl.BlockSpec((tm, tn), lambda j, i: (i, j)),
        out_shape=jax.ShapeDtypeStruct((t, d), BF16),
        compiler_params=_params("arbitrary", "arbitrary"),
        name="merge_branches",
    )(xn, *first, *second, wg, wg, wg, w0, w1, w2)


def _out_router_body(x_ref, m_ref, wo_ref, gf_ref, wr_ref, wrl_ref, br_ref, x1_ref, route_ref, *, n_experts):
    x1 = x_ref[...] + _dot(m_ref[...], wo_ref[...])
    x1_ref[...] = x1
    hn = _rms(x1, gf_ref[...])
    hn_hi = hn.astype(BF16)
    hn_lo = (hn - hn_hi.astype(F32)).astype(BF16)
    lg = (_dot(hn_hi, wr_ref[...]) + (_dot(hn_lo, wr_ref[...]) + _dot(hn_hi, wrl_ref[...]))) + br_ref[...]
    lane = lax.broadcasted_iota(jnp.int32, lg.shape, 1).astype(F32)
    epg = n_experts // N_GROUPS
    is_g = lane < N_GROUPS
    gl = jnp.where(is_g, lg, -jnp.inf)
    gmax = jnp.max(gl, axis=-1, keepdims=True)
    g_sel = jnp.min(jnp.where(gl == gmax, lane, LANES), axis=-1, keepdims=True)
    pg = 1.0 / jnp.sum(jnp.where(is_g, jnp.exp(lg - gmax), 0.0), axis=-1, keepdims=True)
    lo = N_GROUPS + g_sel * epg
    el = jnp.where((lane >= lo) & (lane < lo + epg), lg, -jnp.inf)
    t1 = jnp.max(el, axis=-1, keepdims=True)
    i1 = jnp.min(jnp.where(el == t1, lane, LANES), axis=-1, keepdims=True)
    el2 = jnp.where(lane == i1, -jnp.inf, el)
    t2 = jnp.max(el2, axis=-1, keepdims=True)
    i2 = jnp.min(jnp.where(el2 == t2, lane, LANES), axis=-1, keepdims=True)
    dlt = jnp.exp(t2 - t1)
    w1 = pg / (1.0 + dlt)
    w2 = pg * dlt / (1.0 + dlt)
    e1 = i1 - N_GROUPS
    e2 = i2 - N_GROUPS
    route = jnp.where(lane == 0, e1, jnp.where(lane == 1, e2, jnp.where(lane == 2, w1, jnp.where(lane == 3, w2, 0.0))))
    route_ref[...] = route


def out_router(x, merged, wo, gf, wr, wr_lo, br, n_experts, tm_pref=512):
    t, d = x.shape
    tm = _tile(t, tm_pref, 16)
    row = pl.BlockSpec((tm, d), lambda i: (i, 0))
    return pl.pallas_call(
        functools.partial(_out_router_body, n_experts=n_experts),
        grid=(t // tm,),
        in_specs=[row, row,
                  pl.BlockSpec((d, d), lambda i: (0, 0)),
                  pl.BlockSpec((1, d), lambda i: (0, 0)),
                  pl.BlockSpec((d, LANES), lambda i: (0, 0)),
                  pl.BlockSpec((d, LANES), lambda i: (0, 0)),
                  pl.BlockSpec((1, LANES), lambda i: (0, 0))],
        out_specs=[row, pl.BlockSpec((tm, LANES), lambda i: (i, 0))],
        out_shape=[jax.ShapeDtypeStruct((t, d), F32), jax.ShapeDtypeStruct((t, LANES), F32)],
        compiler_params=_params("arbitrary"),
        name="out_router",
    )(x, merged, wo, gf, wr, wr_lo, br)


def _gmm_body(src_ref, te_ref, tv_ref, x_hbm, gf_ref, wgu_ref, wd_ref, y_ref, buf, sem, *, tm, d_expert):
    del te_ref
    i = pl.program_id(0)
    n = pl.num_programs(0)

    def fetch(tile, slot):
        def issue(r, carry):
            pltpu.make_async_copy(x_hbm.at[pl.ds(src_ref[tile * tm + r], 1)], buf.at[slot, pl.ds(r, 1)],
                                  sem.at[slot]).start()
            return carry

        lax.fori_loop(0, tm, issue, 0, unroll=8)

    @pl.when(jnp.logical_and(i == 0, tv_ref[0] > 0))
    def _():
        fetch(0, 0)

    @pl.when(jnp.logical_and(i + 1 < n, tv_ref[jnp.minimum(i + 1, n - 1)] > 0))
    def _():
        fetch(i + 1, (i + 1) % 2)

    valid = tv_ref[i] > 0

    @pl.when(valid)
    def _():
        slot = i % 2
        pltpu.make_async_copy(x_hbm.at[pl.ds(0, tm)], buf.at[slot], sem.at[slot]).wait()
        hn = _rms(buf[slot], gf_ref[...]).astype(BF16)
        au = _dot(hn, wgu_ref[0])
        a = au[:, :d_expert]
        act = (a * _sigmoid(a) * au[:, d_expert:]).astype(BF16)
        y_ref[...] = _dot(act, wd_ref[0])

    @pl.when(jnp.logical_not(valid))
    def _():
        y_ref[...] = jnp.zeros_like(y_ref)


def grouped_mlp(x, src, gf, wgu, wd, tile_expert, tile_valid, tm):
    d = x.shape[1]
    p = src.shape[0]
    d_expert = wd.shape[1]
    return pl.pallas_call(
        functools.partial(_gmm_body, tm=tm, d_expert=d_expert),
        grid_spec=pltpu.PrefetchScalarGridSpec(
            num_scalar_prefetch=3,
            grid=(p // tm,),
            in_specs=[
                pl.BlockSpec(memory_space=pl.ANY),
                pl.BlockSpec((1, d), lambda i, src, te, tv: (0, 0)),
                pl.BlockSpec((1, d, 2 * d_expert), lambda i, src, te, tv: (te[i], 0, 0)),
                pl.BlockSpec((1, d_expert, d), lambda i, src, te, tv: (te[i], 0, 0)),
            ],
            out_specs=pl.BlockSpec((tm, d), lambda i, src, te, tv: (i, 0)),
            scratch_shapes=[pltpu.VMEM((2, tm, d), F32), pltpu.SemaphoreType.DMA((2,))],
        ),
        out_shape=jax.ShapeDtypeStruct((p, d), F32),
        compiler_params=_params("arbitrary"),
        name="moe_grouped_mlp",
    )(src, tile_expert, tile_valid, x, gf, wgu, wd)


def _combine_body(pos_ref, x1_ref, route_ref, gn_ref, y_hbm, o_ref, buf, sem, *, tm, final_norm):
    i = pl.program_id(0)
    n = pl.num_programs(0)
    n_copies = TOP_K * tm

    def fetch(tile, slot):
        def issue(r, carry):
            for k in range(TOP_K):
                src_row = pos_ref[(tile * tm + r) * TOP_K + k]
                pltpu.make_async_copy(y_hbm.at[pl.ds(src_row, 1)], buf.at[slot, pl.ds(k * tm + r, 1)],
                                      sem.at[slot]).start()
            return carry

        lax.fori_loop(0, tm, issue, 0, unroll=4)

    @pl.when(i == 0)
    def _():
        fetch(0, 0)

    @pl.when(i + 1 < n)
    def _():
        fetch(i + 1, (i + 1) % 2)

    slot = i % 2
    pltpu.make_async_copy(y_hbm.at[pl.ds(0, n_copies)], buf.at[slot], sem.at[slot]).wait()
    rt = route_ref[...]
    x2 = x1_ref[...] + (rt[:, 2:3] * buf[slot, 0:tm, :] + rt[:, 3:4] * buf[slot, tm:2 * tm, :])
    if final_norm:
        x2 = _rms(x2, gn_ref[...])
    o_ref[...] = x2


def combine(x1, route, y, pos, gn, final_norm, tm_pref=256):
    t, d = x1.shape
    tm = _tile(t, tm_pref, SUBLANES)
    return pl.pallas_call(
        functools.partial(_combine_body, tm=tm, final_norm=final_norm),
        grid_spec=pltpu.PrefetchScalarGridSpec(
            num_scalar_prefetch=1,
            grid=(t // tm,),
            in_specs=[
                pl.BlockSpec((tm, d), lambda i, pos: (i, 0)),
                pl.BlockSpec((tm, LANES), lambda i, pos: (i, 0)),
                pl.BlockSpec((1, d), lambda i, pos: (0, 0)),
                pl.BlockSpec(memory_space=pl.ANY),
            ],
            out_specs=pl.BlockSpec((tm, d), lambda i, pos: (i, 0)),
            scratch_shapes=[pltpu.VMEM((2, TOP_K * tm, d), F32), pltpu.SemaphoreType.DMA((2,))],
        ),
        out_shape=jax.ShapeDtypeStruct((t, d), F32),
        compiler_params=_params("arbitrary"),
        name="moe_combine",
    )(pos, x1, route, gn, y)


def _dispatch_plan(route, n_experts, tm):
    t = route.shape[0]
    n_assign = TOP_K * t
    eid = route[:, :TOP_K].astype(jnp.int32).reshape(n_assign)
    onehot = (eid[:, None] == jnp.arange(n_experts, dtype=jnp.int32)[None, :]).astype(jnp.int32)
    csum = jnp.cumsum(onehot, axis=0)
    rank = jnp.sum((csum - onehot) * onehot, axis=1)
    counts = csum[-1]
    padded = ((counts + tm - 1) // tm) * tm
    ends = jnp.cumsum(padded)
    starts = ends - padded
    pos = (starts[eid] + rank).astype(jnp.int32)
    n_tiles = (n_assign + n_experts * (tm - 1)) // tm
    src = jnp.zeros((n_tiles * tm,), jnp.int32).at[pos].set(jnp.arange(n_assign, dtype=jnp.int32) // TOP_K)
    tile_start = jnp.arange(n_tiles, dtype=jnp.int32) * tm
    tile_expert = jnp.sum((tile_start[:, None] >= ends[None, :]).astype(jnp.int32), axis=1)
    tile_expert = jnp.minimum(tile_expert, n_experts - 1)
    tile_valid = (tile_start < ends[-1]).astype(jnp.int32)
    return pos, src, tile_expert, tile_valid


def hier_moe(x1, route, gf, wgu, wd, gn, final_norm, tm=256):
    n_experts = wgu.shape[0]
    pos, src, tile_expert, tile_valid = _dispatch_plan(route, n_experts, tm)
    y = grouped_mlp(x1, src, gf, wgu, wd, tile_expert, tile_valid, tm)
    return combine(x1, route, y, pos, gn, final_norm)


def _rope_tables(pos):
    half = ROPE_DIM // 2
    inv_freq = ROPE_THETA ** (-jnp.arange(half, dtype=F32) / half)
    ang = pos.astype(F32)[:, None] * inv_freq[None, :]
    z = jnp.zeros((pos.shape[0], LANES - ROPE_DIM), F32)
    cos, sin = jnp.cos(ang), jnp.sin(ang)
    return jnp.concatenate([cos, cos, z], axis=1), jnp.concatenate([sin, sin, z], axis=1)


def _swap_halves(w):
    half = ROPE_DIM // 2
    return jnp.concatenate([-w[..., half:], w[..., :half]], axis=-1)


def kernel(x_prompt, x_sample, mem_prompt, cache_mla_ckv, cache_mla_krope, cache_mem_k, cache_mem_v, state_rglru_h, state_rglru_conv, norm_mix, w_in, conv_w, conv_b, lru_wa, lru_ba, lru_wx, lru_bx, lru_lambda, mla_q_norm, mla_w_uq, mla_kv_norm, mla_w_uk, mla_w_uv, mem_norm, mem_w_k, mem_w_v, w_branch_rec, w_branch_mla, w_branch_mem, w_out, norm_ffn, router_group_w, router_group_b, router_expert_w, router_expert_b, moe_w_gate, moe_w_up, moe_w_down, norm_final):
    bp, sp, d = x_prompt.shape
    bs, ss, _ = x_sample.shape
    depth = w_in.shape[0]
    past = cache_mla_ckv.shape[2]
    m_tok = mem_prompt.shape[1]
    d_rnn = conv_w.shape[2]
    q_lora = mla_q_norm.shape[1]
    kv_lora = mla_kv_norm.shape[1]
    n_heads = mla_w_uq.shape[2]
    mem_heads = cache_mem_k.shape[3]
    mem_q = mem_heads * cache_mem_k.shape[4]
    n_experts = moe_w_gate.shape[1]
    tp, tsm = bp * sp, bs * ss
    scale = (NOPE_DIM + ROPE_DIM) ** -0.5

    x = jnp.concatenate([x_prompt.reshape(tp, d), x_sample.reshape(tsm, d)], axis=0)
    pos_all = jnp.concatenate([jnp.tile(jnp.arange(sp), bp), jnp.tile(past + jnp.arange(ss), bs)])
    cos, sin = _rope_tables(pos_all)
    mem_flat = mem_prompt.reshape(bp * m_tok, d)
    cache_mem_k = cache_mem_k.reshape(depth, bs, m_tok, mem_q)
    cache_mem_v = cache_mem_v.reshape(depth, bs, m_tok, mem_q)
    conv_zero = jnp.zeros((bp, CONV_W - 1, d_rnn), F32)
    h_zero = jnp.zeros((bp, 1, d_rnn), F32)
    row2 = lambda v: v.reshape(1, -1)

    outs = {k: [] for k in ("p_ckv", "p_kr", "p_mk", "p_mv", "p_h", "p_conv", "s_ckv", "s_kr", "s_h", "s_conv")}
    for l in range(depth):
        o = 0
        segs = []
        for wdt in (d_rnn, d_rnn, q_lora, kv_lora, ROPE_DIM, mem_q, 3 * d):
            segs.append(w_in[l][:, o:o + wdt])
            o += wdt
        w_ux, w_uy, w_cq, w_ckv, w_kr, w_qm, w_g = segs
        w_xym = jnp.concatenate([w_ux, w_uy, w_qm], axis=1).astype(BF16)
        w_cq = w_cq.astype(BF16)
        w_kv = jnp.concatenate([w_ckv, w_kr, _swap_halves(w_kr)], axis=1).astype(BF16)
        w_g = w_g.astype(BF16)
        uq = mla_w_uq[l]
        uq_rope = uq[..., NOPE_DIM:]
        w_uq = jnp.concatenate([uq, _swap_halves(uq_rope)], axis=-1).reshape(q_lora, n_heads * HEAD_PAD).astype(BF16)
        w_uk = mla_w_uk[l].reshape(kv_lora, n_heads * NOPE_DIM).astype(BF16)
        w_uv = mla_w_uv[l].reshape(kv_lora, n_heads * V_DIM).astype(BF16)
        w_ukt = jnp.transpose(mla_w_uk[l], (1, 2, 0)).astype(BF16)
        w_uvh = jnp.transpose(mla_w_uv[l], (1, 0, 2)).astype(BF16)
        w_memkv = jnp.concatenate([mem_w_k[l], mem_w_v[l]], axis=1).astype(BF16)
        w_r32 = jnp.concatenate([router_group_w[l], router_expert_w[l],
                                 jnp.zeros((d, LANES - N_GROUPS - n_experts), F32)], axis=1)
        w_r = w_r32.astype(BF16)
        w_r_lo = (w_r32 - w_r.astype(F32)).astype(BF16)
        b_r = jnp.concatenate([router_group_b[l], router_expert_b[l],
                               jnp.zeros((LANES - N_GROUPS - n_experts,), F32)]).reshape(1, LANES)
        w_gu = jnp.concatenate([moe_w_gate[l], moe_w_up[l]], axis=-1).astype(BF16)
        w_dn = moe_w_down[l].astype(BF16)

        mkv, _ = norm_matmul(mem_flat, row2(mem_norm[l]), w_memkv, F32)
        mk = mkv[:, :mem_q].reshape(bp, m_tok, mem_q)
        mv = mkv[:, mem_q:].reshape(bp, m_tok, mem_q)
        outs["p_mk"].append(mk.reshape(bp, m_tok, mem_heads, -1))
        outs["p_mv"].append(mv.reshape(bp, m_tok, mem_heads, -1))

        z, xn = norm_matmul(x, row2(norm_mix[l]), w_xym, BF16)

        lru = (conv_w[l], row2(conv_b[l]), lru_wa[l].astype(BF16), lru_wx[l].astype(BF16),
               row2(lru_ba[l]), row2(lru_bx[l]), row2(lru_lambda[l]))
        rec_p, hl_p, cb_p = rglru(z, 0, bp, sp, conv_zero, h_zero, *lru, d_rnn)
        rec_s, hl_s, cb_s = rglru(z, tp, bs, ss, state_rglru_conv[l], state_rglru_h[l].reshape(bs, 1, d_rnn),
                                  *lru, d_rnn)
        outs["p_h"].append(hl_p.reshape(bp, d_rnn)); outs["p_conv"].append(cb_p)
        outs["s_h"].append(hl_s.reshape(bs, d_rnn)); outs["s_conv"].append(cb_s)

        q = q_proj(xn, w_cq, row2(mla_q_norm[l]), w_uq, cos, sin, n_heads, scale)
        ckv, kr, k, v = kv_proj(xn, w_kv, row2(mla_kv_norm[l]), w_uk, w_uv, cos, sin, n_heads)
        outs["p_ckv"].append(ckv[:tp].reshape(bp, sp, kv_lora)); outs["p_kr"].append(kr[:tp].reshape(bp, sp, ROPE_DIM))
        outs["s_ckv"].append(ckv[tp:].reshape(bs, ss, kv_lora)); outs["s_kr"].append(kr[tp:].reshape(bs, ss, ROPE_DIM))
        mla_p = attn_prompt(q, k, v, bp, sp)
        mla_s = attn_cached(q, ckv, kr, cache_mla_ckv, cache_mla_krope, l, w_ukt, w_uvh, tp, bs, ss)

        col_blk = (2 * d_rnn) // mem_q
        mem_p = mem_attn(z, col_blk, mk[None], mv[None], 0, 0, bp, sp, mem_heads)
        mem_s = mem_attn(z, col_blk, cache_mem_k, cache_mem_v, l, tp, bs, ss, mem_heads)

        merged = merge_branches(xn, (rec_p, mla_p, mem_p), (rec_s, mla_s, mem_s), w_g,
                                w_branch_rec[l].astype(BF16),
                                w_branch_mla[l].astype(BF16), w_branch_mem[l].astype(BF16))
        x1, route = out_router(x, merged, w_out[l].astype(BF16), row2(norm_ffn[l]), w_r, w_r_lo, b_r, n_experts)
        x = hier_moe(x1, route, row2(norm_ffn[l]), w_gu, w_dn, row2(norm_final), l == depth - 1)

    st = lambda name: jnp.stack(outs[name])
    return (x[:tp].reshape(bp, sp, d), x[tp:].reshape(bs, ss, d),
            st("p_ckv"), st("p_kr"), st("p_mk"), st("p_mv"), st("p_h"), st("p_conv"),
            st("s_ckv"), st("s_kr"), st("s_h"), st("s_conv"))
```

```python
import functools

import jax
import jax.numpy as jnp
from jax import lax
from jax.experimental import pallas as pl
from jax.experimental.pallas import tpu as pltpu

F32 = jnp.float32
BF16 = jnp.bfloat16

EPS = 1e-6
CHUNK = 64
LRU_C = 8.0
ROPE_THETA = 10000.0
CONV_W = 4
NOPE_DIM = 128
ROPE_DIM = 64
V_DIM = 128
N_GROUPS = 4
TOP_K = 2

LANES = 128
SUBLANES = 8
HEAD_PAD = 2 * LANES
V7X_VMEM_BYTES = 64 * 1024 * 1024
VMEM_LIMIT = (V7X_VMEM_BYTES * 7) // 8
NEG_BIG = -1e30
LOG2_E = 1.4426950408889634


def _params(*sem):
    return pltpu.CompilerParams(dimension_semantics=sem, vmem_limit_bytes=VMEM_LIMIT)


def _tile(n, pref, mult=SUBLANES):
    t = min(pref, n)
    t -= t % mult
    while n % t:
        t -= mult
    return t


def _rms(x, g):
    y = x * lax.rsqrt(jnp.mean(x * x, axis=-1, keepdims=True) + EPS)
    return y * g


def _gelu(x):
    return 0.5 * x * (1.0 + jnp.tanh(0.7978845608028654 * (x + 0.044715 * (x * x * x))))


def _sigmoid(x):
    return 1.0 / (1.0 + jnp.exp(-x))


def _dot(a, b):
    return jnp.dot(a, b, preferred_element_type=F32)


def _dot_t(a, b):
    return lax.dot_general(a, b, (((1,), (1,)), ((), ())), preferred_element_type=F32)


def _norm_matmul_body(x_ref, g_ref, w_ref, z_ref, xn_ref, xn_scr):
    @pl.when(pl.program_id(1) == 0)
    def _():
        xn = _rms(x_ref[...], g_ref[...]).astype(BF16)
        xn_scr[...] = xn
        xn_ref[...] = xn

    z_ref[...] = _dot(xn_scr[...], w_ref[...]).astype(z_ref.dtype)


def norm_matmul(x, g, w, out_dtype, tm_pref=512, tn_pref=1024):
    t, d = x.shape
    n = w.shape[1]
    tm, tn = _tile(t, tm_pref, 16), _tile(n, tn_pref, LANES)
    return pl.pallas_call(
        _norm_matmul_body,
        grid=(t // tm, n // tn),
        in_specs=[
            pl.BlockSpec((tm, d), lambda i, j: (i, 0)),
            pl.BlockSpec((1, d), lambda i, j: (0, 0)),
            pl.BlockSpec((d, tn), lambda i, j: (0, j)),
        ],
        out_specs=[
            pl.BlockSpec((tm, tn), lambda i, j: (i, j)),
            pl.BlockSpec((tm, d), lambda i, j: (i, 0)),
        ],
        out_shape=[jax.ShapeDtypeStruct((t, n), out_dtype), jax.ShapeDtypeStruct((t, d), BF16)],
        scratch_shapes=[pltpu.VMEM((tm, d), BF16)],
        compiler_params=_params("arbitrary", "arbitrary"),
        name="norm_matmul",
    )(x, g, w)


def _rope_mix(r, c, s):
    return r * c + pltpu.roll(r, ROPE_DIM, 1) * s


def _q_proj_body(xn_ref, wq_ref, gq_ref, wuq_ref, cos_ref, sin_ref, q_ref, *, n_heads, scale):
    cq = _dot(xn_ref[...], wq_ref[...])
    cqn = _rms(cq, gq_ref[...]).astype(BF16)
    c = cos_ref[...] * scale
    s = sin_ref[...] * scale
    for h in range(n_heads):
        qh = _dot(cqn, wuq_ref[:, h * HEAD_PAD:(h + 1) * HEAD_PAD])
        q_ref[h, :, :NOPE_DIM] = (qh[:, :NOPE_DIM] * scale).astype(BF16)
        q_ref[h, :, NOPE_DIM:] = _rope_mix(qh[:, NOPE_DIM:], c, s).astype(BF16)


def q_proj(xn, wq, gq, wuq, cos, sin, n_heads, scale, tm_pref=512):
    t, d = xn.shape
    ql = wq.shape[1]
    tm = _tile(t, tm_pref, 16)
    return pl.pallas_call(
        functools.partial(_q_proj_body, n_heads=n_heads, scale=scale),
        grid=(t // tm,),
        in_specs=[
            pl.BlockSpec((tm, d), lambda i: (i, 0)),
            pl.BlockSpec((d, ql), lambda i: (0, 0)),
            pl.BlockSpec((1, ql), lambda i: (0, 0)),
            pl.BlockSpec((ql, n_heads * HEAD_PAD), lambda i: (0, 0)),
            pl.BlockSpec((tm, LANES), lambda i: (i, 0)),
            pl.BlockSpec((tm, LANES), lambda i: (i, 0)),
        ],
        out_specs=pl.BlockSpec((n_heads, tm, HEAD_PAD), lambda i: (0, i, 0)),
        out_shape=jax.ShapeDtypeStruct((n_heads, t, HEAD_PAD), BF16),
        compiler_params=_params("arbitrary"),
        name="q_proj",
    )(xn, wq, gq, wuq, cos, sin)


def _kv_proj_body(xn_ref, wkv_ref, gkv_ref, wuk_ref, wuv_ref, cos_ref, sin_ref,
                  ckv_ref, kr_ref, k_ref, v_ref, *, n_heads, kv_lora):
    ckr = _dot(xn_ref[...], wkv_ref[...])
    ckv = _rms(ckr[:, :kv_lora], gkv_ref[...])
    ckv_ref[...] = ckv
    rr = _rope_mix(ckr[:, kv_lora:], cos_ref[...], sin_ref[...])
    kr_ref[...] = rr[:, :ROPE_DIM]
    ckv_b = ckv.astype(BF16)
    rr_b = rr.astype(BF16)
    kn = _dot(ckv_b, wuk_ref[...])
    vv = _dot(ckv_b, wuv_ref[...])
    for h in range(n_heads):
        k_ref[h, :, :NOPE_DIM] = kn[:, h * NOPE_DIM:(h + 1) * NOPE_DIM].astype(BF16)
        k_ref[h, :, NOPE_DIM:] = rr_b
        v_ref[h] = vv[:, h * V_DIM:(h + 1) * V_DIM].astype(BF16)


def kv_proj(xn, wkv, gkv, wuk, wuv, cos, sin, n_heads, tm_pref=512):
    t, d = xn.shape
    kv_lora = gkv.shape[1]
    tm = _tile(t, tm_pref, 16)
    return pl.pallas_call(
        functools.partial(_kv_proj_body, n_heads=n_heads, kv_lora=kv_lora),
        grid=(t // tm,),
        in_specs=[
            pl.BlockSpec((tm, d), lambda i: (i, 0)),
            pl.BlockSpec((d, kv_lora + LANES), lambda i: (0, 0)),
            pl.BlockSpec((1, kv_lora), lambda i: (0, 0)),
            pl.BlockSpec((kv_lora, n_heads * NOPE_DIM), lambda i: (0, 0)),
            pl.BlockSpec((kv_lora, n_heads * V_DIM), lambda i: (0, 0)),
            pl.BlockSpec((tm, LANES), lambda i: (i, 0)),
            pl.BlockSpec((tm, LANES), lambda i: (i, 0)),
        ],
        out_specs=[
            pl.BlockSpec((tm, kv_lora), lambda i: (i, 0)),
            pl.BlockSpec((tm, ROPE_DIM), lambda i: (i, 0)),
            pl.BlockSpec((n_heads, tm, HEAD_PAD), lambda i: (0, i, 0)),
            pl.BlockSpec((n_heads, tm, V_DIM), lambda i: (0, i, 0)),
        ],
        out_shape=[
            jax.ShapeDtypeStruct((t, kv_lora), F32),
            jax.ShapeDtypeStruct((t, ROPE_DIM), F32),
            jax.ShapeDtypeStruct((n_heads, t, HEAD_PAD), BF16),
            jax.ShapeDtypeStruct((n_heads, t, V_DIM), BF16),
        ],
        compiler_params=_params("arbitrary"),
        name="kv_proj",
    )(xn, wkv, gkv, wuk, wuv, cos, sin)


def _rglru_body(ux_ref, uy_ref, conv0_ref, h0_ref, cw_ref, cb_ref, wa_ref, wx_ref, ba_ref, bx_ref,
                lam_ref, o_ref, hlast_ref, nconv_ref, ubuf, a_scr, b_scr, hcar, *, ts, n_blocks):
    hist = SUBLANES
    n_col = a_scr.shape[0]
    d = n_col * LANES
    col = lambda c: slice(c * LANES, (c + 1) * LANES)

    @pl.when(pl.program_id(1) == 0)
    def _():
        ubuf[hist - (CONV_W - 1):hist, :] = conv0_ref[0]
        hcar[...] = h0_ref[0]

    ubuf[hist:hist + ts, :] = ux_ref[...].astype(F32)
    cw = cw_ref[...]
    xc = cb_ref[...]
    for k in range(CONV_W):
        off = hist - (CONV_W - 1) + k
        xc = xc + ubuf[off:off + ts, :] * cw[k:k + 1, :]
    tail = ubuf[hist + ts - (CONV_W - 1):hist + ts, :]
    nconv_ref[0] = tail
    ubuf[hist - (CONV_W - 1):hist, :] = tail

    lam = lam_ref[...]
    sp = jnp.maximum(-lam, 0.0) + jnp.log(1.0 + jnp.exp(-jnp.abs(lam)))
    for n in range(n_blocks):
        sl = col(n)
        xcn = xc[:, sl]
        xb = xcn.astype(BF16)
        r = _sigmoid(_dot(xb, wa_ref[n]) + ba_ref[:, sl])
        i = _sigmoid(_dot(xb, wx_ref[n]) + bx_ref[:, sl])
        a = jnp.exp((-LRU_C) * r * sp[:, sl])
        a_scr[n] = a
        b_scr[n] = jnp.sqrt(1.0 - a * a) * (i * xcn)

    seg = ts // SUBLANES

    def step(t, carry):
        h, p = carry
        rows = pl.ds(t, SUBLANES, stride=seg)
        hs, ps = [], []
        for c in range(n_col):
            a = a_scr[c, rows, :]
            hc = a * h[:, col(c)] + b_scr[c, rows, :]
            pc = a * p[:, col(c)]
            b_scr[c, rows, :] = hc
            a_scr[c, rows, :] = pc
            hs.append(hc)
            ps.append(pc)
        return jnp.concatenate(hs, axis=1), jnp.concatenate(ps, axis=1)

    h, p = lax.fori_loop(0, seg, step, (jnp.zeros((SUBLANES, d), F32), jnp.ones((SUBLANES, d), F32)))
    carry_in = hcar[...]
    for s in range(SUBLANES):
        rows = slice(s * seg, (s + 1) * seg)
        for c in range(n_col):
            b_scr[c, rows, :] = b_scr[c, rows, :] + a_scr[c, rows, :] * carry_in[:, col(c)]
        carry_in = h[s:s + 1, :] + p[s:s + 1, :] * carry_in
    hcar[...] = carry_in
    hlast_ref[0] = carry_in
    for c in range(n_col):
        o_ref[:, col(c)] = (b_scr[c] * _gelu(uy_ref[:, col(c)].astype(F32))).astype(o_ref.dtype)


def rglru(z, row0, n_batch, seq, conv0, h0, cw, cb, wa, wx, ba, bx, lam, d_rnn, ts_pref=256):
    ts = _tile(seq, ts_pref, 16)
    nst = seq // ts
    blk0 = row0 // ts
    n_blocks = wa.shape[0]
    assert d_rnn == n_blocks * LANES and ts % SUBLANES == 0
    vec = pl.BlockSpec((1, d_rnn), lambda b, s: (0, 0))
    in_specs = [
        pl.BlockSpec((ts, d_rnn), lambda b, s: (blk0 + b * nst + s, 0)),
        pl.BlockSpec((ts, d_rnn), lambda b, s: (blk0 + b * nst + s, 1)),
        pl.BlockSpec((1, CONV_W - 1, d_rnn), lambda b, s: (b, 0, 0)),
        pl.BlockSpec((1, 1, d_rnn), lambda b, s: (b, 0, 0)),
        pl.BlockSpec((CONV_W, d_rnn), lambda b, s: (0, 0)),
        vec,
        pl.BlockSpec(wa.shape, lambda b, s: (0, 0, 0)),
        pl.BlockSpec(wx.shape, lambda b, s: (0, 0, 0)),
        vec, vec, vec,
    ]
    return pl.pallas_call(
        functools.partial(_rglru_body, ts=ts, n_blocks=n_blocks),
        grid=(n_batch, nst),
        in_specs=in_specs,
        out_specs=[
            pl.BlockSpec((ts, d_rnn), lambda b, s: (b * nst + s, 0)),
            pl.BlockSpec((1, 1, d_rnn), lambda b, s: (b, 0, 0)),
            pl.BlockSpec((1, CONV_W - 1, d_rnn), lambda b, s: (b, 0, 0)),
        ],
        out_shape=[
            jax.ShapeDtypeStruct((n_batch * seq, d_rnn), BF16),
            jax.ShapeDtypeStruct((n_batch, 1, d_rnn), F32),
            jax.ShapeDtypeStruct((n_batch, CONV_W - 1, d_rnn), F32),
        ],
        scratch_shapes=[
            pltpu.VMEM((SUBLANES + ts, d_rnn), F32),
            pltpu.VMEM((d_rnn // LANES, ts, LANES), F32),
            pltpu.VMEM((d_rnn // LANES, ts, LANES), F32),
            pltpu.VMEM((1, d_rnn), F32),
        ],
        compiler_params=_params("arbitrary", "arbitrary"),
        name="rglru",
    )(z, z, conv0, h0, cw, cb, wa, wx, ba, bx, lam)


def _softmax_step(s, v, m, l, acc):
    m_new = jnp.maximum(m, jnp.max(s, axis=-1, keepdims=True))
    alpha = jnp.exp2(m - m_new)
    p = jnp.exp2(s - m_new)
    l = alpha * l + jnp.sum(p, axis=-1, keepdims=True)
    acc = alpha * acc + _dot(p.astype(BF16), v)
    return m_new, l, acc


def _attn_prompt_body(q_ref, k_ref, v_ref, o_ref, *, tq, tk, hpb):
    qi = pl.program_id(2)
    qs = [q_ref[h] for h in range(hpb)]

    def kv(h, j):
        off = pl.multiple_of(j * tk, tk)
        return k_ref[h, pl.ds(off, tk), :], v_ref[h, pl.ds(off, tk), :]

    def body(j, carry):
        out = []
        for h in range(hpb):
            k, v = kv(h, j)
            out.append(_softmax_step(_dot_t(qs[h], k), v, *carry[h]))
        return tuple(out)

    init = (jnp.full((tq, 1), NEG_BIG, F32), jnp.zeros((tq, 1), F32), jnp.zeros((tq, V_DIM), F32))
    n_full = (qi * tq) // tk
    carry = lax.fori_loop(0, n_full, body, (init,) * hpb)
    shift = CHUNK.bit_length() - 1
    off = pl.multiple_of(n_full * tk, tq)
    q_off = qi * tq - off

    def boundary(width):
        row_chunk = (lax.broadcasted_iota(jnp.int32, (tq, width), 0) + q_off) >> shift
        col_chunk = lax.broadcasted_iota(jnp.int32, (tq, width), 1) >> shift
        visible = col_chunk <= row_chunk
        for h in range(hpb):
            k = k_ref[h, pl.ds(off, width), :]
            v = v_ref[h, pl.ds(off, width), :]
            s = jnp.where(visible, _dot_t(qs[h], k), NEG_BIG)
            m, l, acc = _softmax_step(s, v, *carry[h])
            o_ref[:, h * V_DIM:(h + 1) * V_DIM] = (acc / l).astype(o_ref.dtype)

    if tk == tq:
        boundary(tq)
    else:
        @pl.when(q_off == 0)
        def _():
            boundary(tq)

        @pl.when(q_off != 0)
        def _():
            boundary(tk)


def attn_prompt(q, k, v, n_batch, seq, tq_pref=512, tk_pref=1024, hpb=2):
    n_heads = q.shape[0]
    assert n_heads % hpb == 0
    tq = _tile(seq, tq_pref, CHUNK)
    tk = _tile(seq, tk_pref, tq)
    nq = seq // tq
    return pl.pallas_call(
        functools.partial(_attn_prompt_body, tq=tq, tk=tk, hpb=hpb),
        grid=(n_batch, n_heads // hpb, nq),
        in_specs=[
            pl.BlockSpec((hpb, tq, HEAD_PAD), lambda b, h, i: (h, b * nq + i, 0)),
            pl.BlockSpec((hpb, seq, HEAD_PAD), lambda b, h, i: (h, b, 0)),
            pl.BlockSpec((hpb, seq, V_DIM), lambda b, h, i: (h, b, 0)),
        ],
        out_specs=pl.BlockSpec((tq, hpb * V_DIM), lambda b, h, i: (b * nq + i, h)),
        out_shape=jax.ShapeDtypeStruct((n_batch * seq, n_heads * V_DIM), BF16),
        compiler_params=_params("arbitrary", "arbitrary", "arbitrary"),
        name="attn_prompt",
    )(q, k, v)


def _attn_cached_body(q_ref, ckvn_ref, krn_ref, cckv_ref, ckr_ref, wukt_ref, wuv_ref, o_ref,
                      ql_scr, qr_scr, *, n_heads, sq, tk, past):
    for h in range(n_heads):
        qh = q_ref[h]
        ql_scr[h * sq:(h + 1) * sq, :] = _dot(qh[:, :NOPE_DIM], wukt_ref[h]).astype(BF16)
        qr_scr[h * sq:(h + 1) * sq, :] = qh[:, NOPE_DIM:NOPE_DIM + ROPE_DIM]
    ql = ql_scr[...]
    qr = qr_scr[...]
    rows = n_heads * sq
    kv_lora = ql.shape[1]

    def body(j, carry):
        off = pl.multiple_of(j * tk, tk)
        kc = cckv_ref[0, 0, pl.ds(off, tk), :].astype(BF16)
        kr = ckr_ref[0, 0, pl.ds(off, tk), :].astype(BF16)
        return _softmax_step(_dot_t(ql, kc) + _dot_t(qr, kr), kc, *carry)

    init = (jnp.full((rows, 1), NEG_BIG, F32), jnp.zeros((rows, 1), F32), jnp.zeros((rows, kv_lora), F32))
    carry = lax.fori_loop(0, past // tk, body, init)
    kc = ckvn_ref[...].astype(BF16)
    kr = krn_ref[...].astype(BF16)
    m, l, acc = _softmax_step(_dot_t(ql, kc) + _dot_t(qr, kr), kc, *carry)
    o_lat = (acc / l).astype(BF16)
    for h in range(n_heads):
        o_ref[:, h * V_DIM:(h + 1) * V_DIM] = _dot(o_lat[h * sq:(h + 1) * sq, :], wuv_ref[h]).astype(o_ref.dtype)


def attn_cached(q, ckv, kr, cache_ckv, cache_kr, layer, wukt, wuv, row0, n_batch, sq, tk_pref=512):
    n_heads = q.shape[0]
    kv_lora = ckv.shape[1]
    past = cache_ckv.shape[2]
    tk = _tile(past, tk_pref, 16)
    blk0 = row0 // sq
    return pl.pallas_call(
        functools.partial(_attn_cached_body, n_heads=n_heads, sq=sq, tk=tk, past=past),
        grid=(n_batch,),
        in_specs=[
            pl.BlockSpec((n_heads, sq, HEAD_PAD), lambda b: (0, blk0 + b, 0)),
            pl.BlockSpec((sq, kv_lora), lambda b: (blk0 + b, 0)),
            pl.BlockSpec((sq, ROPE_DIM), lambda b: (blk0 + b, 0)),
            pl.BlockSpec((1, 1, past, kv_lora), lambda b: (layer, b, 0, 0)),
            pl.BlockSpec((1, 1, past, ROPE_DIM), lambda b: (layer, b, 0, 0)),
            pl.BlockSpec(wukt.shape, lambda b: (0, 0, 0)),
            pl.BlockSpec(wuv.shape, lambda b: (0, 0, 0)),
        ],
        out_specs=pl.BlockSpec((sq, n_heads * V_DIM), lambda b: (b, 0)),
        out_shape=jax.ShapeDtypeStruct((n_batch * sq, n_heads * V_DIM), BF16),
        scratch_shapes=[pltpu.VMEM((n_heads * sq, kv_lora), BF16), pltpu.VMEM((n_heads * sq, ROPE_DIM), BF16)],
        compiler_params=_params("arbitrary"),
        name="attn_cached",
    )(q, ckv, kr, cache_ckv, cache_kr, wukt, wuv)


def _mem_attn_body(q_ref, k_ref, v_ref, o_ref, *, n_heads):
    hd = q_ref.shape[1] // n_heads
    scale = hd ** -0.5
    for h in range(n_heads):
        sl = slice(h * hd, (h + 1) * hd)
        s = _dot_t(q_ref[:, sl], k_ref[0, 0, :, h, :].astype(BF16)) * scale
        e = jnp.exp(s - jnp.max(s, axis=-1, keepdims=True))
        o = _dot(e.astype(BF16), v_ref[0, 0, :, h, :].astype(BF16)) / jnp.sum(e, axis=-1, keepdims=True)
        o_ref[:, sl] = o.astype(o_ref.dtype)


def mem_attn(z, col_blk, mem_k, mem_v, layer, row0, n_batch, seq, tm_pref=512):
    _, _, m_tok, n_heads, hd = mem_k.shape
    dq = n_heads * hd
    tm = _tile(seq, tm_pref, 16)
    nst = seq // tm
    blk0 = row0 // tm
    mem_spec = pl.BlockSpec((1, 1, m_tok, n_heads, hd), lambda b, s: (layer, b, 0, 0, 0))
    return pl.pallas_call(
        functools.partial(_mem_attn_body, n_heads=n_heads),
        grid=(n_batch, nst),
        in_specs=[
            pl.BlockSpec((tm, dq), lambda b, s: (blk0 + b * nst + s, col_blk)),
            mem_spec,
            mem_spec,
        ],
        out_specs=pl.BlockSpec((tm, dq), lambda b, s: (b * nst + s, 0)),
        out_shape=jax.ShapeDtypeStruct((n_batch * seq, dq), BF16),
        compiler_params=_params("arbitrary", "arbitrary"),
        name="mem_attn",
    )(z, mem_k, mem_v)


def _merge_body(xn_ref, p0_ref, p1_ref, p2_ref, s0_ref, s1_ref, s2_ref,
                g0_ref, g1_ref, g2_ref, w0_ref, w1_ref, w2_ref, o_ref, *, n_first):
    def emit(a0_ref, a1_ref, a2_ref):
        xn = xn_ref[...]
        acc = _sigmoid(_dot(xn, g0_ref[...])) * _dot(a0_ref[...], w0_ref[...])
        acc = acc + _sigmoid(_dot(xn, g1_ref[...])) * _dot(a1_ref[...], w1_ref[...])
        acc = acc + _sigmoid(_dot(xn, g2_ref[...])) * _dot(a2_ref[...], w2_ref[...])
        o_ref[...] = acc.astype(o_ref.dtype)

    i = pl.program_id(1)

    @pl.when(i < n_first)
    def _():
        emit(p0_ref, p1_ref, p2_ref)

    @pl.when(i >= n_first)
    def _():
        emit(s0_ref, s1_ref, s2_ref)


def merge_branches(xn, first, second, wg, w0, w1, w2, tm_pref=256, tn_pref=512):
    t, d = xn.shape
    t1, t2 = first[0].shape[0], second[0].shape[0]
    tm, tn = _tile(min(t1, t2), tm_pref, 16), _tile(d, tn_pref, LANES)
    assert t1 % tm == 0 and t2 % tm == 0 and t1 + t2 == t
    n1 = t1 // tm
    nj = d // tn
    act = pl.BlockSpec((tm, d), lambda j, i: (i, 0))
    act1 = pl.BlockSpec((tm, d), lambda j, i: (jnp.minimum(i, n1 - 1), 0))
    act2 = pl.BlockSpec((tm, d), lambda j, i: (jnp.maximum(i - n1, 0), 0))
    wsp = pl.BlockSpec((d, tn), lambda j, i: (0, j))
    return pl.pallas_call(
        functools.partial(_merge_body, n_first=n1),
        grid=(nj, t // tm),
        in_specs=[act, act1, act1, act1, act2, act2, act2,
                  pl.BlockSpec((d, tn), lambda j, i: (0, j)),
                  pl.BlockSpec((d, tn), lambda j, i: (0, nj + j)),
                  pl.BlockSpec((d, tn), lambda j, i: (0, 2 * nj + j)),
                  wsp, wsp, wsp],
        out_specs=pl.BlockSpec((tm, tn), lambda j, i: (i, j)),
        out_shape=jax.ShapeDtypeStruct((t, d), BF16),
        compiler_params=_params("arbitrary", "arbitrary"),
        name="merge_branches",
    )(xn, *first, *second, wg, wg, wg, w0, w1, w2)


def _out_router_body(x_ref, m_ref, wo_ref, gf_ref, wr_ref, wrl_ref, br_ref, x1_ref, route_ref, *, n_experts):
    x1 = x_ref[...] + _dot(m_ref[...], wo_ref[...])
    x1_ref[...] = x1
    hn = _rms(x1, gf_ref[...])
    hn_hi = hn.astype(BF16)
    hn_lo = (hn - hn_hi.astype(F32)).astype(BF16)
    lg = (_dot(hn_hi, wr_ref[...]) + (_dot(hn_lo, wr_ref[...]) + _dot(hn_hi, wrl_ref[...]))) + br_ref[...]
    lane = lax.broadcasted_iota(jnp.int32, lg.shape, 1).astype(F32)
    epg = n_experts // N_GROUPS
    is_g = lane < N_GROUPS
    gl = jnp.where(is_g, lg, -jnp.inf)
    gmax = jnp.max(gl, axis=-1, keepdims=True)
    g_sel = jnp.min(jnp.where(gl == gmax, lane, LANES), axis=-1, keepdims=True)
    pg = 1.0 / jnp.sum(jnp.where(is_g, jnp.exp(lg - gmax), 0.0), axis=-1, keepdims=True)
    lo = N_GROUPS + g_sel * epg
    el = jnp.where((lane >= lo) & (lane < lo + epg), lg, -jnp.inf)
    t1 = jnp.max(el, axis=-1, keepdims=True)
    i1 = jnp.min(jnp.where(el == t1, lane, LANES), axis=-1, keepdims=True)
    el2 = jnp.where(lane == i1, -jnp.inf, el)
    t2 = jnp.max(el2, axis=-1, keepdims=True)
    i2 = jnp.min(jnp.where(el2 == t2, lane, LANES), axis=-1, keepdims=True)
    dlt = jnp.exp(t2 - t1)
    w1 = pg / (1.0 + dlt)
    w2 = pg * dlt / (1.0 + dlt)
    e1 = i1 - N_GROUPS
    e2 = i2 - N_GROUPS
    route = jnp.where(lane == 0, e1, jnp.where(lane == 1, e2, jnp.where(lane == 2, w1, jnp.where(lane == 3, w2, 0.0))))
    route_ref[...] = route


def out_router(x, merged, wo, gf, wr, wr_lo, br, n_experts, tm_pref=512):
    t, d = x.shape
    tm = _tile(t, tm_pref, 16)
    row = pl.BlockSpec((tm, d), lambda i: (i, 0))
    return pl.pallas_call(
        functools.partial(_out_router_body, n_experts=n_experts),
        grid=(t // tm,),
        in_specs=[row, row,
                  pl.BlockSpec((d, d), lambda i: (0, 0)),
                  pl.BlockSpec((1, d), lambda i: (0, 0)),
                  pl.BlockSpec((d, LANES), lambda i: (0, 0)),
                  pl.BlockSpec((d, LANES), lambda i: (0, 0)),
                  pl.BlockSpec((1, LANES), lambda i: (0, 0))],
        out_specs=[row, pl.BlockSpec((tm, LANES), lambda i: (i, 0))],
        out_shape=[jax.ShapeDtypeStruct((t, d), F32), jax.ShapeDtypeStruct((t, LANES), F32)],
        compiler_params=_params("arbitrary"),
        name="out_router",
    )(x, merged, wo, gf, wr, wr_lo, br)


def _gmm_body(src_ref, te_ref, tv_ref, x_hbm, gf_ref, wg_ref, wu_ref, wd_ref, y_ref,
              buf, sem, wg_b, wu_b, wd_b, *, tm):
    i = pl.program_id(0)
    n = pl.num_programs(0)

    def fetch(tile, slot):
        def issue(r, carry):
            pltpu.make_async_copy(x_hbm.at[pl.ds(src_ref[tile * tm + r], 1)], buf.at[slot, pl.ds(r, 1)],
                                  sem.at[slot]).start()
            return carry

        lax.fori_loop(0, tm, issue, 0, unroll=8)

    @pl.when(jnp.logical_and(i == 0, tv_ref[0] > 0))
    def _():
        fetch(0, 0)

    @pl.when(jnp.logical_and(i + 1 < n, tv_ref[jnp.minimum(i + 1, n - 1)] > 0))
    def _():
        fetch(i + 1, (i + 1) % 2)

    valid = tv_ref[i] > 0
    new_expert = jnp.logical_or(i == 0, te_ref[i] != te_ref[jnp.maximum(i - 1, 0)])

    @pl.when(jnp.logical_and(valid, new_expert))
    def _():
        wg_b[...] = wg_ref[0, 0].astype(BF16)
        wu_b[...] = wu_ref[0, 0].astype(BF16)
        wd_b[...] = wd_ref[0, 0].astype(BF16)

    @pl.when(valid)
    def _():
        slot = i % 2
        pltpu.make_async_copy(x_hbm.at[pl.ds(0, tm)], buf.at[slot], sem.at[slot]).wait()
        hn = _rms(buf[slot], gf_ref[...]).astype(BF16)
        a = _dot(hn, wg_b[...])
        act = (a * _sigmoid(a) * _dot(hn, wu_b[...])).astype(BF16)
        y_ref[...] = _dot(act, wd_b[...])

    @pl.when(jnp.logical_not(valid))
    def _():
        y_ref[...] = jnp.zeros_like(y_ref)


def grouped_mlp(x, src, gf, w_gate, w_up, w_down, layer, tile_expert, tile_valid, tm):
    d = x.shape[1]
    p = src.shape[0]
    d_expert = w_down.shape[2]
    w_in_spec = pl.BlockSpec((1, 1, d, d_expert), lambda i, src, te, tv: (layer, te[i], 0, 0))
    return pl.pallas_call(
        functools.partial(_gmm_body, tm=tm),
        grid_spec=pltpu.PrefetchScalarGridSpec(
            num_scalar_prefetch=3,
            grid=(p // tm,),
            in_specs=[
                pl.BlockSpec(memory_space=pl.ANY),
                pl.BlockSpec((1, d), lambda i, src, te, tv: (0, 0)),
                w_in_spec,
                w_in_spec,
                pl.BlockSpec((1, 1, d_expert, d), lambda i, src, te, tv: (layer, te[i], 0, 0)),
            ],
            out_specs=pl.BlockSpec((tm, d), lambda i, src, te, tv: (i, 0)),
            scratch_shapes=[pltpu.VMEM((2, tm, d), F32), pltpu.SemaphoreType.DMA((2,)),
                            pltpu.VMEM((d, d_expert), BF16), pltpu.VMEM((d, d_expert), BF16),
                            pltpu.VMEM((d_expert, d), BF16)],
        ),
        out_shape=jax.ShapeDtypeStruct((p, d), F32),
        compiler_params=_params("arbitrary"),
        name="moe_grouped_mlp",
    )(src, tile_expert, tile_valid, x, gf, w_gate, w_up, w_down)


def _combine_body(pos_ref, x1_ref, route_ref, gn_ref, y_hbm, o_ref, buf, sem, *, tm, final_norm):
    i = pl.program_id(0)
    n = pl.num_programs(0)
    n_copies = TOP_K * tm

    def fetch(tile, slot):
        def issue(r, carry):
            for k in range(TOP_K):
                src_row = pos_ref[(tile * tm + r) * TOP_K + k]
                pltpu.make_async_copy(y_hbm.at[pl.ds(src_row, 1)], buf.at[slot, pl.ds(k * tm + r, 1)],
                                      sem.at[slot]).start()
            return carry

        lax.fori_loop(0, tm, issue, 0, unroll=4)

    @pl.when(i == 0)
    def _():
        fetch(0, 0)

    @pl.when(i + 1 < n)
    def _():
        fetch(i + 1, (i + 1) % 2)

    slot = i % 2
    pltpu.make_async_copy(y_hbm.at[pl.ds(0, n_copies)], buf.at[slot], sem.at[slot]).wait()
    rt = route_ref[...]
    x2 = x1_ref[...] + (rt[:, 2:3] * buf[slot, 0:tm, :] + rt[:, 3:4] * buf[slot, tm:2 * tm, :])
    if final_norm:
        x2 = _rms(x2, gn_ref[...])
    o_ref[...] = x2


def combine(x1, route, y, pos, gn, final_norm, tm_pref=256):
    t, d = x1.shape
    tm = _tile(t, tm_pref, SUBLANES)
    return pl.pallas_call(
        functools.partial(_combine_body, tm=tm, final_norm=final_norm),
        grid_spec=pltpu.PrefetchScalarGridSpec(
            num_scalar_prefetch=1,
            grid=(t // tm,),
            in_specs=[
                pl.BlockSpec((tm, d), lambda i, pos: (i, 0)),
                pl.BlockSpec((tm, LANES), lambda i, pos: (i, 0)),
                pl.BlockSpec((1, d), lambda i, pos: (0, 0)),
                pl.BlockSpec(memory_space=pl.ANY),
            ],
            out_specs=pl.BlockSpec((tm, d), lambda i, pos: (i, 0)),
            scratch_shapes=[pltpu.VMEM((2, TOP_K * tm, d), F32), pltpu.SemaphoreType.DMA((2,))],
        ),
        out_shape=jax.ShapeDtypeStruct((t, d), F32),
        compiler_params=_params("arbitrary"),
        name="moe_combine",
    )(pos, x1, route, gn, y)


def _dispatch_plan(route, n_experts, tm):
    t = route.shape[0]
    n_assign = TOP_K * t
    eid = route[:, :TOP_K].astype(jnp.int32).reshape(n_assign)
    onehot = (eid[:, None] == jnp.arange(n_experts, dtype=jnp.int32)[None, :]).astype(jnp.int32)
    csum = jnp.cumsum(onehot, axis=0)
    rank = jnp.sum((csum - onehot) * onehot, axis=1)
    counts = csum[-1]
    padded = ((counts + tm - 1) // tm) * tm
    ends = jnp.cumsum(padded)
    starts = ends - padded
    pos = (starts[eid] + rank).astype(jnp.int32)
    n_tiles = (n_assign + n_experts * (tm - 1)) // tm
    src = jnp.zeros((n_tiles * tm,), jnp.int32).at[pos].set(jnp.arange(n_assign, dtype=jnp.int32) // TOP_K)
    tile_start = jnp.arange(n_tiles, dtype=jnp.int32) * tm
    tile_expert = jnp.sum((tile_start[:, None] >= ends[None, :]).astype(jnp.int32), axis=1)
    tile_expert = jnp.minimum(tile_expert, n_experts - 1)
    tile_valid = (tile_start < ends[-1]).astype(jnp.int32)
    return pos, src, tile_expert, tile_valid


def hier_moe(x1, route, gf, w_gate, w_up, w_down, layer, gn, final_norm, tm=256):
    n_experts = w_gate.shape[1]
    pos, src, tile_expert, tile_valid = _dispatch_plan(route, n_experts, tm)
    y = grouped_mlp(x1, src, gf, w_gate, w_up, w_down, layer, tile_expert, tile_valid, tm)
    return combine(x1, route, y, pos, gn, final_norm)


def _rope_tables(pos):
    half = ROPE_DIM // 2
    inv_freq = ROPE_THETA ** (-jnp.arange(half, dtype=F32) / half)
    ang = pos.astype(F32)[:, None] * inv_freq[None, :]
    z = jnp.zeros((pos.shape[0], LANES - ROPE_DIM), F32)
    cos, sin = jnp.cos(ang), jnp.sin(ang)
    return jnp.concatenate([cos, cos, z], axis=1), jnp.concatenate([sin, sin, z], axis=1)


def _swap_halves(w):
    half = ROPE_DIM // 2
    return jnp.concatenate([-w[..., half:], w[..., :half]], axis=-1)


def kernel(x_prompt, x_sample, mem_prompt, cache_mla_ckv, cache_mla_krope, cache_mem_k, cache_mem_v, state_rglru_h, state_rglru_conv, norm_mix, w_in, conv_w, conv_b, lru_wa, lru_ba, lru_wx, lru_bx, lru_lambda, mla_q_norm, mla_w_uq, mla_kv_norm, mla_w_uk, mla_w_uv, mem_norm, mem_w_k, mem_w_v, w_branch_rec, w_branch_mla, w_branch_mem, w_out, norm_ffn, router_group_w, router_group_b, router_expert_w, router_expert_b, moe_w_gate, moe_w_up, moe_w_down, norm_final):
    bp, sp, d = x_prompt.shape
    bs, ss, _ = x_sample.shape
    depth = w_in.shape[0]
    past = cache_mla_ckv.shape[2]
    m_tok = mem_prompt.shape[1]
    d_rnn = conv_w.shape[2]
    q_lora = mla_q_norm.shape[1]
    kv_lora = mla_kv_norm.shape[1]
    n_heads = mla_w_uq.shape[2]
    mem_heads = cache_mem_k.shape[3]
    mem_q = mem_heads * cache_mem_k.shape[4]
    n_experts = moe_w_gate.shape[1]
    tp, tsm = bp * sp, bs * ss
    scale = (NOPE_DIM + ROPE_DIM) ** -0.5 * LOG2_E

    x = jnp.concatenate([x_prompt.reshape(tp, d), x_sample.reshape(tsm, d)], axis=0)
    pos_all = jnp.concatenate([jnp.tile(jnp.arange(sp), bp), jnp.tile(past + jnp.arange(ss), bs)])
    cos, sin = _rope_tables(pos_all)
    mem_flat = mem_prompt.reshape(bp * m_tok, d)
    conv_zero = jnp.zeros((bp, CONV_W - 1, d_rnn), F32)
    h_zero = jnp.zeros((bp, 1, d_rnn), F32)
    row2 = lambda v: v.reshape(1, -1)

    outs = {k: [] for k in ("p_ckv", "p_kr", "p_mk", "p_mv", "p_h", "p_conv", "s_ckv", "s_kr", "s_h", "s_conv")}
    for l in range(depth):
        o = 0
        segs = []
        for wdt in (d_rnn, d_rnn, q_lora, kv_lora, ROPE_DIM, mem_q, 3 * d):
            segs.append(w_in[l][:, o:o + wdt])
            o += wdt
        w_ux, w_uy, w_cq, w_ckv, w_kr, w_qm, w_g = segs
        w_xym = jnp.concatenate([w_ux, w_uy, w_qm], axis=1).astype(BF16)
        w_cq = w_cq.astype(BF16)
        w_kv = jnp.concatenate([w_ckv, w_kr, _swap_halves(w_kr)], axis=1).astype(BF16)
        w_g = w_g.astype(BF16)
        uq = mla_w_uq[l]
        uq_rope = uq[..., NOPE_DIM:]
        w_uq = jnp.concatenate([uq, _swap_halves(uq_rope)], axis=-1).reshape(q_lora, n_heads * HEAD_PAD).astype(BF16)
        w_uk = mla_w_uk[l].reshape(kv_lora, n_heads * NOPE_DIM).astype(BF16)
        w_uv = mla_w_uv[l].reshape(kv_lora, n_heads * V_DIM).astype(BF16)
        w_ukt = jnp.transpose(mla_w_uk[l], (1, 2, 0)).astype(BF16)
        w_uvh = jnp.transpose(mla_w_uv[l], (1, 0, 2)).astype(BF16)
        w_memkv = jnp.concatenate([mem_w_k[l], mem_w_v[l]], axis=1).astype(BF16)
        w_r32 = jnp.concatenate([router_group_w[l], router_expert_w[l],
                                 jnp.zeros((d, LANES - N_GROUPS - n_experts), F32)], axis=1)
        w_r = w_r32.astype(BF16)
        w_r_lo = (w_r32 - w_r.astype(F32)).astype(BF16)
        b_r = jnp.concatenate([router_group_b[l], router_expert_b[l],
                               jnp.zeros((LANES - N_GROUPS - n_experts,), F32)]).reshape(1, LANES)

        mkv, _ = norm_matmul(mem_flat, row2(mem_norm[l]), w_memkv, F32)
        mk = mkv[:, :mem_q].reshape(bp, m_tok, mem_heads, -1)
        mv = mkv[:, mem_q:].reshape(bp, m_tok, mem_heads, -1)
        outs["p_mk"].append(mk)
        outs["p_mv"].append(mv)

        z, xn = norm_matmul(x, row2(norm_mix[l]), w_xym, BF16)

        lru = (conv_w[l], row2(conv_b[l]), lru_wa[l].astype(BF16), lru_wx[l].astype(BF16),
               row2(lru_ba[l]), row2(lru_bx[l]), row2(lru_lambda[l]))
        rec_p, hl_p, cb_p = rglru(z, 0, bp, sp, conv_zero, h_zero, *lru, d_rnn)
        rec_s, hl_s, cb_s = rglru(z, tp, bs, ss, state_rglru_conv[l], state_rglru_h[l].reshape(bs, 1, d_rnn),
                                  *lru, d_rnn)
        outs["p_h"].append(hl_p.reshape(bp, d_rnn)); outs["p_conv"].append(cb_p)
        outs["s_h"].append(hl_s.reshape(bs, d_rnn)); outs["s_conv"].append(cb_s)

        q = q_proj(xn, w_cq, row2(mla_q_norm[l]), w_uq, cos, sin, n_heads, scale)
        ckv, kr, k, v = kv_proj(xn, w_kv, row2(mla_kv_norm[l]), w_uk, w_uv, cos, sin, n_heads)
        outs["p_ckv"].append(ckv[:tp].reshape(bp, sp, kv_lora)); outs["p_kr"].append(kr[:tp].reshape(bp, sp, ROPE_DIM))
        outs["s_ckv"].append(ckv[tp:].reshape(bs, ss, kv_lora)); outs["s_kr"].append(kr[tp:].reshape(bs, ss, ROPE_DIM))
        mla_p = attn_prompt(q, k, v, bp, sp)
        mla_s = attn_cached(q, ckv, kr, cache_mla_ckv, cache_mla_krope, l, w_ukt, w_uvh, tp, bs, ss)

        col_blk = (2 * d_rnn) // mem_q
        mem_p = mem_attn(z, col_blk, mk[None], mv[None], 0, 0, bp, sp)
        mem_s = mem_attn(z, col_blk, cache_mem_k, cache_mem_v, l, tp, bs, ss)

        merged = merge_branches(xn, (rec_p, mla_p, mem_p), (rec_s, mla_s, mem_s), w_g,
                                w_branch_rec[l].astype(BF16),
                                w_branch_mla[l].astype(BF16), w_branch_mem[l].astype(BF16))
        x1, route = out_router(x, merged, w_out[l].astype(BF16), row2(norm_ffn[l]), w_r, w_r_lo, b_r, n_experts)
        x = hier_moe(x1, route, row2(norm_ffn[l]), moe_w_gate, moe_w_up, moe_w_down, l,
                     row2(norm_final), l == depth - 1)

    st = lambda name: jnp.stack(outs[name])
    return (x[:tp].reshape(bp, sp, d), x[tp:].reshape(bs, ss, d),
            st("p_ckv"), st("p_kr"), st("p_mk"), st("p_mv"), st("p_h"), st("p_conv"),
            st("s_ckv"), st("s_kr"), st("s_h"), st("s_conv"))
```

```python
import functools

import jax
import jax.numpy as jnp
from jax import lax
from jax.experimental import pallas as pl
from jax.experimental.pallas import tpu as pltpu

F32 = jnp.float32
BF16 = jnp.bfloat16

EPS = 1e-6
CHUNK = 64
LRU_C = 8.0
ROPE_THETA = 10000.0
CONV_W = 4
NOPE_DIM = 128
ROPE_DIM = 64
V_DIM = 128
N_GROUPS = 4
TOP_K = 2

LANES = 128
SUBLANES = 8
HEAD_PAD = 2 * LANES
V7X_VMEM_BYTES = 64 * 1024 * 1024
VMEM_LIMIT = (V7X_VMEM_BYTES * 7) // 8
NEG_BIG = -1e30
LOG2_E = 1.4426950408889634


def _params(*sem):
    return pltpu.CompilerParams(dimension_semantics=sem, vmem_limit_bytes=VMEM_LIMIT)


def _tile(n, pref, mult=SUBLANES):
    t = min(pref, n)
    t -= t % mult
    while n % t:
        t -= mult
    return t


def _rms(x, g):
    y = x * lax.rsqrt(jnp.mean(x * x, axis=-1, keepdims=True) + EPS)
    return y * g


def _gelu(x):
    return 0.5 * x * (1.0 + jnp.tanh(0.7978845608028654 * (x + 0.044715 * (x * x * x))))


def _sigmoid(x):
    return 1.0 / (1.0 + jnp.exp(-x))


def _dot(a, b):
    return jnp.dot(a, b, preferred_element_type=F32)


def _dot_t(a, b):
    return lax.dot_general(a, b, (((1,), (1,)), ((), ())), preferred_element_type=F32)


def _norm_matmul_body(x_ref, g_ref, w_ref, z_ref, xn_ref, xn_scr):
    @pl.when(pl.program_id(1) == 0)
    def _():
        xn = _rms(x_ref[...], g_ref[...]).astype(BF16)
        xn_scr[...] = xn
        xn_ref[...] = xn

    z_ref[...] = _dot(xn_scr[...], w_ref[...]).astype(z_ref.dtype)


def norm_matmul(x, g, w, out_dtype, tm_pref=512, tn_pref=1024):
    t, d = x.shape
    n = w.shape[1]
    tm, tn = _tile(t, tm_pref, 16), _tile(n, tn_pref, LANES)
    return pl.pallas_call(
        _norm_matmul_body,
        grid=(t // tm, n // tn),
        in_specs=[
            pl.BlockSpec((tm, d), lambda i, j: (i, 0)),
            pl.BlockSpec((1, d), lambda i, j: (0, 0)),
            pl.BlockSpec((d, tn), lambda i, j: (0, j)),
        ],
        out_specs=[
            pl.BlockSpec((tm, tn), lambda i, j: (i, j)),
            pl.BlockSpec((tm, d), lambda i, j: (i, 0)),
        ],
        out_shape=[jax.ShapeDtypeStruct((t, n), out_dtype), jax.ShapeDtypeStruct((t, d), BF16)],
        scratch_shapes=[pltpu.VMEM((tm, d), BF16)],
        compiler_params=_params("arbitrary", "arbitrary"),
        name="norm_matmul",
    )(x, g, w)


def _rope_mix(r, c, s):
    return r * c + pltpu.roll(r, ROPE_DIM, 1) * s


def _q_proj_body(xn_ref, wq_ref, gq_ref, wuq_ref, cos_ref, sin_ref, q_ref, *, n_heads, scale):
    cq = _dot(xn_ref[...], wq_ref[...])
    cqn = _rms(cq, gq_ref[...]).astype(BF16)
    c = cos_ref[...] * scale
    s = sin_ref[...] * scale
    for h in range(n_heads):
        qh = _dot(cqn, wuq_ref[:, h * HEAD_PAD:(h + 1) * HEAD_PAD])
        q_ref[h, :, :NOPE_DIM] = (qh[:, :NOPE_DIM] * scale).astype(BF16)
        q_ref[h, :, NOPE_DIM:] = _rope_mix(qh[:, NOPE_DIM:], c, s).astype(BF16)


def q_proj(xn, wq, gq, wuq, cos, sin, n_heads, scale, tm_pref=512):
    t, d = xn.shape
    ql = wq.shape[1]
    tm = _tile(t, tm_pref, 16)
    return pl.pallas_call(
        functools.partial(_q_proj_body, n_heads=n_heads, scale=scale),
        grid=(t // tm,),
        in_specs=[
            pl.BlockSpec((tm, d), lambda i: (i, 0)),
            pl.BlockSpec((d, ql), lambda i: (0, 0)),
            pl.BlockSpec((1, ql), lambda i: (0, 0)),
            pl.BlockSpec((ql, n_heads * HEAD_PAD), lambda i: (0, 0)),
            pl.BlockSpec((tm, LANES), lambda i: (i, 0)),
            pl.BlockSpec((tm, LANES), lambda i: (i, 0)),
        ],
        out_specs=pl.BlockSpec((n_heads, tm, HEAD_PAD), lambda i: (0, i, 0)),
        out_shape=jax.ShapeDtypeStruct((n_heads, t, HEAD_PAD), BF16),
        compiler_params=_params("arbitrary"),
        name="q_proj",
    )(xn, wq, gq, wuq, cos, sin)


def _kv_proj_body(xn_ref, wkv_ref, gkv_ref, wuk_ref, wuv_ref, cos_ref, sin_ref,
                  ckv_ref, kr_ref, k_ref, v_ref, *, n_heads, kv_lora):
    ckr = _dot(xn_ref[...], wkv_ref[...])
    ckv = _rms(ckr[:, :kv_lora], gkv_ref[...])
    ckv_ref[...] = ckv
    rr = _rope_mix(ckr[:, kv_lora:], cos_ref[...], sin_ref[...])
    kr_ref[...] = rr[:, :ROPE_DIM]
    ckv_b = ckv.astype(BF16)
    rr_b = rr.astype(BF16)
    kn = _dot(ckv_b, wuk_ref[...])
    vv = _dot(ckv_b, wuv_ref[...])
    for h in range(n_heads):
        k_ref[h, :, :NOPE_DIM] = kn[:, h * NOPE_DIM:(h + 1) * NOPE_DIM].astype(BF16)
        k_ref[h, :, NOPE_DIM:] = rr_b
        v_ref[h] = vv[:, h * V_DIM:(h + 1) * V_DIM].astype(BF16)


def kv_proj(xn, wkv, gkv, wuk, wuv, cos, sin, n_heads, tm_pref=512):
    t, d = xn.shape
    kv_lora = gkv.shape[1]
    tm = _tile(t, tm_pref, 16)
    return pl.pallas_call(
        functools.partial(_kv_proj_body, n_heads=n_heads, kv_lora=kv_lora),
        grid=(t // tm,),
        in_specs=[
            pl.BlockSpec((tm, d), lambda i: (i, 0)),
            pl.BlockSpec((d, kv_lora + LANES), lambda i: (0, 0)),
            pl.BlockSpec((1, kv_lora), lambda i: (0, 0)),
            pl.BlockSpec((kv_lora, n_heads * NOPE_DIM), lambda i: (0, 0)),
            pl.BlockSpec((kv_lora, n_heads * V_DIM), lambda i: (0, 0)),
            pl.BlockSpec((tm, LANES), lambda i: (i, 0)),
            pl.BlockSpec((tm, LANES), lambda i: (i, 0)),
        ],
        out_specs=[
            pl.BlockSpec((tm, kv_lora), lambda i: (i, 0)),
            pl.BlockSpec((tm, ROPE_DIM), lambda i: (i, 0)),
            pl.BlockSpec((n_heads, tm, HEAD_PAD), lambda i: (0, i, 0)),
            pl.BlockSpec((n_heads, tm, V_DIM), lambda i: (0, i, 0)),
        ],
        out_shape=[
            jax.ShapeDtypeStruct((t, kv_lora), F32),
            jax.ShapeDtypeStruct((t, ROPE_DIM), F32),
            jax.ShapeDtypeStruct((n_heads, t, HEAD_PAD), BF16),
            jax.ShapeDtypeStruct((n_heads, t, V_DIM), BF16),
        ],
        compiler_params=_params("arbitrary"),
        name="kv_proj",
    )(xn, wkv, gkv, wuk, wuv, cos, sin)


def _rglru_body(ux_ref, uy_ref, conv0_ref, h0_ref, cw_ref, cb_ref, wa_ref, wx_ref, ba_ref, bx_ref,
                lam_ref, o_ref, hlast_ref, nconv_ref, ubuf, hcar, *, ts, n_blocks):
    hist = SUBLANES
    col = lambda c: slice(c * LANES, (c + 1) * LANES)
    row_id = lax.broadcasted_iota(jnp.int32, (ts, LANES), 0)

    @pl.when(pl.program_id(1) == 0)
    def _():
        ubuf[hist - (CONV_W - 1):hist, :] = conv0_ref[0]
        hcar[...] = h0_ref[0]

    ubuf[hist:hist + ts, :] = ux_ref[...].astype(F32)
    cw = cw_ref[...]
    xc = cb_ref[...]
    for k in range(CONV_W):
        off = hist - (CONV_W - 1) + k
        xc = xc + ubuf[off:off + ts, :] * cw[k:k + 1, :]
    tail = ubuf[hist + ts - (CONV_W - 1):hist + ts, :]
    nconv_ref[0] = tail
    ubuf[hist - (CONV_W - 1):hist, :] = tail

    lam = lam_ref[...]
    sp = jnp.maximum(-lam, 0.0) + jnp.log(1.0 + jnp.exp(-jnp.abs(lam)))
    for n in range(n_blocks):
        sl = col(n)
        xcn = xc[:, sl]
        xb = xcn.astype(BF16)
        r = _sigmoid(_dot(xb, wa_ref[n]) + ba_ref[:, sl])
        i = _sigmoid(_dot(xb, wx_ref[n]) + bx_ref[:, sl])
        a = jnp.exp((-LRU_C) * r * sp[:, sl])
        b = jnp.sqrt(1.0 - a * a) * (i * xcn)
        shift = 1
        while shift < ts:
            if shift < SUBLANES:
                keep = row_id >= shift
                b = b + a * jnp.where(keep, pltpu.roll(b, shift, 0), 0.0)
                a = a * jnp.where(keep, pltpu.roll(a, shift, 0), 1.0)
            else:
                b = jnp.concatenate([b[:shift], b[shift:] + a[shift:] * b[:-shift]], axis=0)
                a = jnp.concatenate([a[:shift], a[shift:] * a[:-shift]], axis=0)
            shift *= 2
        hfin = b + a * hcar[:, sl]
        hcar[:, sl] = hfin[ts - 1:ts, :]
        o_ref[:, sl] = (hfin * _gelu(uy_ref[:, sl].astype(F32))).astype(o_ref.dtype)
    hlast_ref[0] = hcar[...]


def rglru(z, row0, n_batch, seq, conv0, h0, cw, cb, wa, wx, ba, bx, lam, d_rnn, ts_pref=256):
    ts = _tile(seq, ts_pref, 16)
    nst = seq // ts
    blk0 = row0 // ts
    n_blocks = wa.shape[0]
    assert d_rnn == n_blocks * LANES and ts % SUBLANES == 0
    vec = pl.BlockSpec((1, d_rnn), lambda b, s: (0, 0))
    in_specs = [
        pl.BlockSpec((ts, d_rnn), lambda b, s: (blk0 + b * nst + s, 0)),
        pl.BlockSpec((ts, d_rnn), lambda b, s: (blk0 + b * nst + s, 1)),
        pl.BlockSpec((1, CONV_W - 1, d_rnn), lambda b, s: (b, 0, 0)),
        pl.BlockSpec((1, 1, d_rnn), lambda b, s: (b, 0, 0)),
        pl.BlockSpec((CONV_W, d_rnn), lambda b, s: (0, 0)),
        vec,
        pl.BlockSpec(wa.shape, lambda b, s: (0, 0, 0)),
        pl.BlockSpec(wx.shape, lambda b, s: (0, 0, 0)),
        vec, vec, vec,
    ]
    return pl.pallas_call(
        functools.partial(_rglru_body, ts=ts, n_blocks=n_blocks),
        grid=(n_batch, nst),
        in_specs=in_specs,
        out_specs=[
            pl.BlockSpec((ts, d_rnn), lambda b, s: (b * nst + s, 0)),
            pl.BlockSpec((1, 1, d_rnn), lambda b, s: (b, 0, 0)),
            pl.BlockSpec((1, CONV_W - 1, d_rnn), lambda b, s: (b, 0, 0)),
        ],
        out_shape=[
            jax.ShapeDtypeStruct((n_batch * seq, d_rnn), BF16),
            jax.ShapeDtypeStruct((n_batch, 1, d_rnn), F32),
            jax.ShapeDtypeStruct((n_batch, CONV_W - 1, d_rnn), F32),
        ],
        scratch_shapes=[
            pltpu.VMEM((SUBLANES + ts, d_rnn), F32),
            pltpu.VMEM((1, d_rnn), F32),
        ],
        compiler_params=_params("arbitrary", "arbitrary"),
        name="rglru",
    )(z, z, conv0, h0, cw, cb, wa, wx, ba, bx, lam)


def _softmax_step(s, v, m, l, acc):
    m_new = jnp.maximum(m, jnp.max(s, axis=-1, keepdims=True))
    alpha = jnp.exp2(m - m_new)
    p = jnp.exp2(s - m_new)
    l = alpha * l + jnp.sum(p, axis=-1, keepdims=True)
    acc = alpha * acc + _dot(p.astype(BF16), v)
    return m_new, l, acc


def _attn_prompt_body(q_ref, k_ref, v_ref, o_ref, *, tq, tk, hpb):
    qi = pl.program_id(2)
    qs = [q_ref[h] for h in range(hpb)]

    def kv(h, j):
        off = pl.multiple_of(j * tk, tk)
        return k_ref[h, pl.ds(off, tk), :], v_ref[h, pl.ds(off, tk), :]

    def body(j, carry):
        out = []
        for h in range(hpb):
            k, v = kv(h, j)
            out.append(_softmax_step(_dot_t(qs[h], k), v, *carry[h]))
        return tuple(out)

    init = (jnp.full((tq, 1), NEG_BIG, F32), jnp.zeros((tq, 1), F32), jnp.zeros((tq, V_DIM), F32))
    n_full = (qi * tq) // tk
    carry = lax.fori_loop(0, n_full, body, (init,) * hpb)
    shift = CHUNK.bit_length() - 1
    off = pl.multiple_of(n_full * tk, tq)
    q_off = qi * tq - off

    def boundary(width):
        row_chunk = (lax.broadcasted_iota(jnp.int32, (tq, width), 0) + q_off) >> shift
        col_chunk = lax.broadcasted_iota(jnp.int32, (tq, width), 1) >> shift
        visible = col_chunk <= row_chunk
        for h in range(hpb):
            k = k_ref[h, pl.ds(off, width), :]
            v = v_ref[h, pl.ds(off, width), :]
            s = jnp.where(visible, _dot_t(qs[h], k), NEG_BIG)
            m, l, acc = _softmax_step(s, v, *carry[h])
            o_ref[:, h * V_DIM:(h + 1) * V_DIM] = (acc / l).astype(o_ref.dtype)

    if tk == tq:
        boundary(tq)
    else:
        @pl.when(q_off == 0)
        def _():
            boundary(tq)

        @pl.when(q_off != 0)
        def _():
            boundary(tk)


def attn_prompt(q, k, v, n_batch, seq, tq_pref=512, tk_pref=1024, hpb=2):
    n_heads = q.shape[0]
    assert n_heads % hpb == 0
    tq = _tile(seq, tq_pref, CHUNK)
    tk = _tile(seq, tk_pref, tq)
    nq = seq // tq
    return pl.pallas_call(
        functools.partial(_attn_prompt_body, tq=tq, tk=tk, hpb=hpb),
        grid=(n_batch, n_heads // hpb, nq),
        in_specs=[
            pl.BlockSpec((hpb, tq, HEAD_PAD), lambda b, h, i: (h, b * nq + i, 0)),
            pl.BlockSpec((hpb, seq, HEAD_PAD), lambda b, h, i: (h, b, 0)),
            pl.BlockSpec((hpb, seq, V_DIM), lambda b, h, i: (h, b, 0)),
        ],
        out_specs=pl.BlockSpec((tq, hpb * V_DIM), lambda b, h, i: (b * nq + i, h)),
        out_shape=jax.ShapeDtypeStruct((n_batch * seq, n_heads * V_DIM), BF16),
        compiler_params=_params("arbitrary", "arbitrary", "arbitrary"),
        name="attn_prompt",
    )(q, k, v)


def _attn_cached_body(q_ref, ckvn_ref, krn_ref, cckv_ref, ckr_ref, wukt_ref, wuv_ref, o_ref,
                      ql_scr, qr_scr, *, n_heads, sq, tk, past):
    for h in range(n_heads):
        qh = q_ref[h]
        ql_scr[h * sq:(h + 1) * sq, :] = _dot(qh[:, :NOPE_DIM], wukt_ref[h]).astype(BF16)
        qr_scr[h * sq:(h + 1) * sq, :] = qh[:, NOPE_DIM:NOPE_DIM + ROPE_DIM]
    ql = ql_scr[...]
    qr = qr_scr[...]
    rows = n_heads * sq
    kv_lora = ql.shape[1]

    def body(j, carry):
        off = pl.multiple_of(j * tk, tk)
        kc = cckv_ref[0, 0, pl.ds(off, tk), :].astype(BF16)
        kr = ckr_ref[0, 0, pl.ds(off, tk), :].astype(BF16)
        return _softmax_step(_dot_t(ql, kc) + _dot_t(qr, kr), kc, *carry)

    init = (jnp.full((rows, 1), NEG_BIG, F32), jnp.zeros((rows, 1), F32), jnp.zeros((rows, kv_lora), F32))
    carry = lax.fori_loop(0, past // tk, body, init)
    kc = ckvn_ref[...].astype(BF16)
    kr = krn_ref[...].astype(BF16)
    m, l, acc = _softmax_step(_dot_t(ql, kc) + _dot_t(qr, kr), kc, *carry)
    o_lat = (acc / l).astype(BF16)
    for h in range(n_heads):
        o_ref[:, h * V_DIM:(h + 1) * V_DIM] = _dot(o_lat[h * sq:(h + 1) * sq, :], wuv_ref[h]).astype(o_ref.dtype)


def attn_cached(q, ckv, kr, cache_ckv, cache_kr, layer, wukt, wuv, row0, n_batch, sq, tk_pref=1024):
    n_heads = q.shape[0]
    kv_lora = ckv.shape[1]
    past = cache_ckv.shape[2]
    tk = _tile(past, tk_pref, 16)
    blk0 = row0 // sq
    return pl.pallas_call(
        functools.partial(_attn_cached_body, n_heads=n_heads, sq=sq, tk=tk, past=past),
        grid=(n_batch,),
        in_specs=[
            pl.BlockSpec((n_heads, sq, HEAD_PAD), lambda b: (0, blk0 + b, 0)),
            pl.BlockSpec((sq, kv_lora), lambda b: (blk0 + b, 0)),
            pl.BlockSpec((sq, ROPE_DIM), lambda b: (blk0 + b, 0)),
            pl.BlockSpec((1, 1, past, kv_lora), lambda b: (layer, b, 0, 0)),
            pl.BlockSpec((1, 1, past, ROPE_DIM), lambda b: (layer, b, 0, 0)),
            pl.BlockSpec(wukt.shape, lambda b: (0, 0, 0)),
            pl.BlockSpec(wuv.shape, lambda b: (0, 0, 0)),
        ],
        out_specs=pl.BlockSpec((sq, n_heads * V_DIM), lambda b: (b, 0)),
        out_shape=jax.ShapeDtypeStruct((n_batch * sq, n_heads * V_DIM), BF16),
        scratch_shapes=[pltpu.VMEM((n_heads * sq, kv_lora), BF16), pltpu.VMEM((n_heads * sq, ROPE_DIM), BF16)],
        compiler_params=_params("arbitrary"),
        name="attn_cached",
    )(q, ckv, kr, cache_ckv, cache_kr, wukt, wuv)


def _mem_attn_body(q_ref, *refs, n_heads):
    k_refs, v_refs, o_ref = refs[:n_heads], refs[n_heads:2 * n_heads], refs[2 * n_heads]
    hd = q_ref.shape[1] // n_heads
    scale = hd ** -0.5
    for h in range(n_heads):
        sl = slice(h * hd, (h + 1) * hd)
        s = _dot_t(q_ref[:, sl], k_refs[h][...].astype(BF16)) * scale
        e = jnp.exp(s - jnp.max(s, axis=-1, keepdims=True))
        o = _dot(e.astype(BF16), v_refs[h][...].astype(BF16)) / jnp.sum(e, axis=-1, keepdims=True)
        o_ref[:, sl] = o.astype(o_ref.dtype)


def mem_attn(z, col_blk, mem_k, mem_v, layer, row0, n_batch, seq, n_heads, hd, v_head0=0, tm_pref=512):
    m_tok = mem_k.shape[2]
    dq = n_heads * hd
    tm = _tile(seq, tm_pref, 16)
    nst = seq // tm
    blk0 = row0 // tm

    def head_spec(h):
        return pl.BlockSpec((None, None, m_tok, hd), lambda b, s: (layer, b, 0, h))

    k_specs = [head_spec(h) for h in range(n_heads)]
    v_specs = [head_spec(v_head0 + h) for h in range(n_heads)]
    return pl.pallas_call(
        functools.partial(_mem_attn_body, n_heads=n_heads),
        grid=(n_batch, nst),
        in_specs=[pl.BlockSpec((tm, dq), lambda b, s: (blk0 + b * nst + s, col_blk))] + k_specs + v_specs,
        out_specs=pl.BlockSpec((tm, dq), lambda b, s: (b * nst + s, 0)),
        out_shape=jax.ShapeDtypeStruct((n_batch * seq, dq), BF16),
        compiler_params=_params("arbitrary", "arbitrary"),
        name="mem_attn",
    )(z, *([mem_k] * n_heads), *([mem_v] * n_heads))


def _mem_attn_cached_body(q_ref, k_hbm, v_hbm, o_ref, kbuf, vbuf, sem, *, layer, n_heads):
    b = pl.program_id(0)
    n = pl.num_programs(0)

    def copies(batch, slot):
        out = []
        for h in range(n_heads):
            out.append(pltpu.make_async_copy(k_hbm.at[layer, batch, :, h, :], kbuf.at[slot, h], sem.at[slot]))
            out.append(pltpu.make_async_copy(v_hbm.at[layer, batch, :, h, :], vbuf.at[slot, h], sem.at[slot]))
        return out

    @pl.when(b == 0)
    def _():
        for c in copies(0, 0):
            c.start()

    @pl.when(b + 1 < n)
    def _():
        for c in copies(b + 1, (b + 1) % 2):
            c.start()

    slot = b % 2
    for c in copies(b, slot):
        c.wait()
    hd = kbuf.shape[3]
    scale = hd ** -0.5
    for h in range(n_heads):
        sl = slice(h * hd, (h + 1) * hd)
        s = _dot_t(q_ref[:, sl], kbuf[slot, h].astype(BF16)) * scale
        e = jnp.exp(s - jnp.max(s, axis=-1, keepdims=True))
        o = _dot(e.astype(BF16), vbuf[slot, h].astype(BF16)) / jnp.sum(e, axis=-1, keepdims=True)
        o_ref[:, sl] = o.astype(o_ref.dtype)


def mem_attn_cached(z, col_blk, mem_k, mem_v, layer, row0, n_batch, seq):
    _, _, m_tok, n_heads, hd = mem_k.shape
    dq = n_heads * hd
    blk0 = row0 // seq
    return pl.pallas_call(
        functools.partial(_mem_attn_cached_body, layer=layer, n_heads=n_heads),
        grid=(n_batch,),
        in_specs=[
            pl.BlockSpec((seq, dq), lambda b: (blk0 + b, col_blk)),
            pl.BlockSpec(memory_space=pl.ANY),
            pl.BlockSpec(memory_space=pl.ANY),
        ],
        out_specs=pl.BlockSpec((seq, dq), lambda b: (b, 0)),
        out_shape=jax.ShapeDtypeStruct((n_batch * seq, dq), BF16),
        scratch_shapes=[pltpu.VMEM((2, n_heads, m_tok, hd), F32), pltpu.VMEM((2, n_heads, m_tok, hd), F32),
                        pltpu.SemaphoreType.DMA((2,))],
        compiler_params=_params("arbitrary"),
        name="mem_attn_cached",
    )(z, mem_k, mem_v)


def _merge_body(xn_ref, p0_ref, p1_ref, p2_ref, s0_ref, s1_ref, s2_ref,
                g0_ref, g1_ref, g2_ref, w0_ref, w1_ref, w2_ref, o_ref, *, n_first):
    def emit(a0_ref, a1_ref, a2_ref):
        xn = xn_ref[...]
        acc = _sigmoid(_dot(xn, g0_ref[...])) * _dot(a0_ref[...], w0_ref[...])
        acc = acc + _sigmoid(_dot(xn, g1_ref[...])) * _dot(a1_ref[...], w1_ref[...])
        acc = acc + _sigmoid(_dot(xn, g2_ref[...])) * _dot(a2_ref[...], w2_ref[...])
        o_ref[...] = acc.astype(o_ref.dtype)

    i = pl.program_id(1)

    @pl.when(i < n_first)
    def _():
        emit(p0_ref, p1_ref, p2_ref)

    @pl.when(i >= n_first)
    def _():
        emit(s0_ref, s1_ref, s2_ref)


def merge_branches(xn, first, second, wg, w0, w1, w2, tm_pref=256, tn_pref=512):
    t, d = xn.shape
    t1, t2 = first[0].shape[0], second[0].shape[0]
    tm, tn = _tile(min(t1, t2), tm_pref, 16), _tile(d, tn_pref, LANES)
    assert t1 % tm == 0 and t2 % tm == 0 and t1 + t2 == t
    n1 = t1 // tm
    nj = d // tn
    act = pl.BlockSpec((tm, d), lambda j, i: (i, 0))
    act1 = pl.BlockSpec((tm, d), lambda j, i: (jnp.minimum(i, n1 - 1), 0))
    act2 = pl.BlockSpec((tm, d), lambda j, i: (jnp.maximum(i - n1, 0), 0))
    wsp = pl.BlockSpec((d, tn), lambda j, i: (0, j))
    return pl.pallas_call(
        functools.partial(_merge_body, n_first=n1),
        grid=(nj, t // tm),
        in_specs=[act, act1, act1, act1, act2, act2, act2,
                  pl.BlockSpec((d, tn), lambda j, i: (0, j)),
                  pl.BlockSpec((d, tn), lambda j, i: (0, nj + j)),
                  pl.BlockSpec((d, tn), lambda j, i: (0, 2 * nj + j)),
                  wsp, wsp, wsp],
        out_specs=pl.BlockSpec((tm, tn), lambda j, i: (i, j)),
        out_shape=jax.ShapeDtypeStruct((t, d), BF16),
        compiler_params=_params("arbitrary", "arbitrary"),
        name="merge_branches",
    )(xn, *first, *second, wg, wg, wg, w0, w1, w2)


def _out_router_body(x_ref, m_ref, wo_ref, gf_ref, wr_ref, wrl_ref, br_ref, x1_ref, route_ref, *, n_experts):
    x1 = x_ref[...] + _dot(m_ref[...], wo_ref[...])
    x1_ref[...] = x1
    hn = _rms(x1, gf_ref[...])
    hn_hi = hn.astype(BF16)
    hn_lo = (hn - hn_hi.astype(F32)).astype(BF16)
    lg = (_dot(hn_hi, wr_ref[...]) + (_dot(hn_lo, wr_ref[...]) + _dot(hn_hi, wrl_ref[...]))) + br_ref[...]
    lane = lax.broadcasted_iota(jnp.int32, lg.shape, 1).astype(F32)
    epg = n_experts // N_GROUPS
    is_g = lane < N_GROUPS
    gl = jnp.where(is_g, lg, -jnp.inf)
    gmax = jnp.max(gl, axis=-1, keepdims=True)
    g_sel = jnp.min(jnp.where(gl == gmax, lane, LANES), axis=-1, keepdims=True)
    pg = 1.0 / jnp.sum(jnp.where(is_g, jnp.exp(lg - gmax), 0.0), axis=-1, keepdims=True)
    lo = N_GROUPS + g_sel * epg
    el = jnp.where((lane >= lo) & (lane < lo + epg), lg, -jnp.inf)
    t1 = jnp.max(el, axis=-1, keepdims=True)
    i1 = jnp.min(jnp.where(el == t1, lane, LANES), axis=-1, keepdims=True)
    el2 = jnp.where(lane == i1, -jnp.inf, el)
    t2 = jnp.max(el2, axis=-1, keepdims=True)
    i2 = jnp.min(jnp.where(el2 == t2, lane, LANES), axis=-1, keepdims=True)
    dlt = jnp.exp(t2 - t1)
    w1 = pg / (1.0 + dlt)
    w2 = pg * dlt / (1.0 + dlt)
    e1 = i1 - N_GROUPS
    e2 = i2 - N_GROUPS
    route = jnp.where(lane == 0, e1, jnp.where(lane == 1, e2, jnp.where(lane == 2, w1, jnp.where(lane == 3, w2, 0.0))))
    route_ref[...] = route


def out_router(x, merged, wo, gf, wr, wr_lo, br, n_experts, tm_pref=512):
    t, d = x.shape
    tm = _tile(t, tm_pref, 16)
    row = pl.BlockSpec((tm, d), lambda i: (i, 0))
    return pl.pallas_call(
        functools.partial(_out_router_body, n_experts=n_experts),
        grid=(t // tm,),
        in_specs=[row, row,
                  pl.BlockSpec((d, d), lambda i: (0, 0)),
                  pl.BlockSpec((1, d), lambda i: (0, 0)),
                  pl.BlockSpec((d, LANES), lambda i: (0, 0)),
                  pl.BlockSpec((d, LANES), lambda i: (0, 0)),
                  pl.BlockSpec((1, LANES), lambda i: (0, 0))],
        out_specs=[row, pl.BlockSpec((tm, LANES), lambda i: (i, 0))],
        out_shape=[jax.ShapeDtypeStruct((t, d), F32), jax.ShapeDtypeStruct((t, LANES), F32)],
        compiler_params=_params("arbitrary"),
        name="out_router",
    )(x, merged, wo, gf, wr, wr_lo, br)


def _gmm_body(src_ref, te_ref, tv_ref, x_hbm, gf_ref, wg_ref, wu_ref, wd_ref, y_ref,
              buf, sem, wg_b, wu_b, wd_b, *, tm):
    i = pl.program_id(0)
    n = pl.num_programs(0)

    def fetch(tile, slot):
        def issue(r, carry):
            pltpu.make_async_copy(x_hbm.at[pl.ds(src_ref[tile * tm + r], 1)], buf.at[slot, pl.ds(r, 1)],
                                  sem.at[slot]).start()
            return carry

        lax.fori_loop(0, tm, issue, 0, unroll=8)

    @pl.when(jnp.logical_and(i == 0, tv_ref[0] > 0))
    def _():
        fetch(0, 0)

    @pl.when(jnp.logical_and(i + 1 < n, tv_ref[jnp.minimum(i + 1, n - 1)] > 0))
    def _():
        fetch(i + 1, (i + 1) % 2)

    valid = tv_ref[i] > 0
    new_expert = jnp.logical_or(i == 0, te_ref[i] != te_ref[jnp.maximum(i - 1, 0)])

    @pl.when(jnp.logical_and(valid, new_expert))
    def _():
        wg_b[...] = wg_ref[0, 0].astype(BF16)
        wu_b[...] = wu_ref[0, 0].astype(BF16)
        wd_b[...] = wd_ref[0, 0].astype(BF16)

    @pl.when(valid)
    def _():
        slot = i % 2
        pltpu.make_async_copy(x_hbm.at[pl.ds(0, tm)], buf.at[slot], sem.at[slot]).wait()
        hn = _rms(buf[slot], gf_ref[...]).astype(BF16)
        a = _dot(hn, wg_b[...])
        act = (a * _sigmoid(a) * _dot(hn, wu_b[...])).astype(BF16)
        y_ref[...] = _dot(act, wd_b[...])

    @pl.when(jnp.logical_not(valid))
    def _():
        y_ref[...] = jnp.zeros_like(y_ref)


def grouped_mlp(x, src, gf, w_gate, w_up, w_down, layer, tile_expert, tile_valid, tm):
    d = x.shape[1]
    p = src.shape[0]
    d_expert = w_down.shape[2]
    w_in_spec = pl.BlockSpec((1, 1, d, d_expert), lambda i, src, te, tv: (layer, te[i], 0, 0))
    return pl.pallas_call(
        functools.partial(_gmm_body, tm=tm),
        grid_spec=pltpu.PrefetchScalarGridSpec(
            num_scalar_prefetch=3,
            grid=(p // tm,),
            in_specs=[
                pl.BlockSpec(memory_space=pl.ANY),
                pl.BlockSpec((1, d), lambda i, src, te, tv: (0, 0)),
                w_in_spec,
                w_in_spec,
                pl.BlockSpec((1, 1, d_expert, d), lambda i, src, te, tv: (layer, te[i], 0, 0)),
            ],
            out_specs=pl.BlockSpec((tm, d), lambda i, src, te, tv: (i, 0)),
            scratch_shapes=[pltpu.VMEM((2, tm, d), F32), pltpu.SemaphoreType.DMA((2,)),
                            pltpu.VMEM((d, d_expert), BF16), pltpu.VMEM((d, d_expert), BF16),
                            pltpu.VMEM((d_expert, d), BF16)],
        ),
        out_shape=jax.ShapeDtypeStruct((p, d), F32),
        compiler_params=_params("arbitrary"),
        name="moe_grouped_mlp",
    )(src, tile_expert, tile_valid, x, gf, w_gate, w_up, w_down)


def _combine_body(pos_ref, x1_ref, route_ref, gn_ref, y_hbm, *rest, tm, final_norm, n_first):
    outs, (buf, sem) = rest[:-2], rest[-2:]
    i = pl.program_id(0)
    n = pl.num_programs(0)
    n_copies = TOP_K * tm

    def fetch(tile, slot):
        def issue(r, carry):
            for k in range(TOP_K):
                src_row = pos_ref[(tile * tm + r) * TOP_K + k]
                pltpu.make_async_copy(y_hbm.at[pl.ds(src_row, 1)], buf.at[slot, pl.ds(k * tm + r, 1)],
                                      sem.at[slot]).start()
            return carry

        lax.fori_loop(0, tm, issue, 0, unroll=4)

    @pl.when(i == 0)
    def _():
        fetch(0, 0)

    @pl.when(i + 1 < n)
    def _():
        fetch(i + 1, (i + 1) % 2)

    slot = i % 2
    pltpu.make_async_copy(y_hbm.at[pl.ds(0, n_copies)], buf.at[slot], sem.at[slot]).wait()
    rt = route_ref[...]
    x2 = x1_ref[...] + (rt[:, 2:3] * buf[slot, 0:tm, :] + rt[:, 3:4] * buf[slot, tm:2 * tm, :])
    if final_norm:
        x2 = _rms(x2, gn_ref[...])
    if n_first is None:
        outs[0][...] = x2
    else:
        @pl.when(i < n_first)
        def _():
            outs[0][...] = x2

        @pl.when(i >= n_first)
        def _():
            outs[1][...] = x2


def combine(x1, route, y, pos, gn, final_norm, split=None, tm_pref=256):
    t, d = x1.shape
    tm = _tile(t if split is None else min(split), tm_pref, SUBLANES)
    if split is None:
        n_first = None
        out_specs = pl.BlockSpec((tm, d), lambda i, pos: (i, 0))
        out_shape = jax.ShapeDtypeStruct((t, d), F32)
    else:
        assert split[0] % tm == 0 and split[1] % tm == 0 and sum(split) == t
        n_first = split[0] // tm
        out_specs = [pl.BlockSpec((tm, d), lambda i, pos: (jnp.minimum(i, n_first - 1), 0)),
                     pl.BlockSpec((tm, d), lambda i, pos: (jnp.maximum(i - n_first, 0), 0))]
        out_shape = [jax.ShapeDtypeStruct((split[0], d), F32), jax.ShapeDtypeStruct((split[1], d), F32)]
    return pl.pallas_call(
        functools.partial(_combine_body, tm=tm, final_norm=final_norm, n_first=n_first),
        grid_spec=pltpu.PrefetchScalarGridSpec(
            num_scalar_prefetch=1,
            grid=(t // tm,),
            in_specs=[
                pl.BlockSpec((tm, d), lambda i, pos: (i, 0)),
                pl.BlockSpec((tm, LANES), lambda i, pos: (i, 0)),
                pl.BlockSpec((1, d), lambda i, pos: (0, 0)),
                pl.BlockSpec(memory_space=pl.ANY),
            ],
            out_specs=out_specs,
            scratch_shapes=[pltpu.VMEM((2, TOP_K * tm, d), F32), pltpu.SemaphoreType.DMA((2,))],
        ),
        out_shape=out_shape,
        compiler_params=_params("arbitrary"),
        name="moe_combine",
    )(pos, x1, route, gn, y)


def _dispatch_plan(route, n_experts, tm):
    t = route.shape[0]
    n_assign = TOP_K * t
    eid = route[:, :TOP_K].astype(jnp.int32).reshape(n_assign)
    onehot = eid[:, None] == jnp.arange(n_experts, dtype=jnp.int32)[None, :]
    blk = _tile(n_assign, LANES, 1)
    n_blk = n_assign // blk
    oh = onehot.astype(BF16).reshape(n_blk, blk, n_experts)
    below = (jnp.arange(blk)[:, None] > jnp.arange(blk)[None, :]).astype(BF16)
    within = jnp.einsum("ij,bje->bie", below, oh, preferred_element_type=F32)
    blk_sum = jnp.sum(oh.astype(F32), axis=1)
    blk_off = jnp.cumsum(blk_sum, axis=0) - blk_sum
    rank_all = (within + blk_off[:, None, :]).reshape(n_assign, n_experts)
    counts = jnp.sum(blk_sum, axis=0).astype(jnp.int32)
    padded = ((counts + tm - 1) // tm) * tm
    ends = jnp.cumsum(padded)
    starts = ends - padded
    pos = jnp.sum(jnp.where(onehot, rank_all + starts.astype(F32)[None, :], 0.0), axis=1).astype(jnp.int32)
    n_tiles = (n_assign + n_experts * (tm - 1)) // tm
    src = jnp.zeros((n_tiles * tm,), jnp.int32).at[pos].set(jnp.arange(n_assign, dtype=jnp.int32) // TOP_K)
    tile_start = jnp.arange(n_tiles, dtype=jnp.int32) * tm
    tile_expert = jnp.sum((tile_start[:, None] >= ends[None, :]).astype(jnp.int32), axis=1)
    tile_expert = jnp.minimum(tile_expert, n_experts - 1)
    tile_valid = (tile_start < ends[-1]).astype(jnp.int32)
    return pos, src, tile_expert, tile_valid


def hier_moe(x1, route, gf, w_gate, w_up, w_down, layer, gn, final_norm, split, tm=256):
    n_experts = w_gate.shape[1]
    pos, src, tile_expert, tile_valid = _dispatch_plan(route, n_experts, tm)
    y = grouped_mlp(x1, src, gf, w_gate, w_up, w_down, layer, tile_expert, tile_valid, tm)
    return combine(x1, route, y, pos, gn, final_norm, split)


def _rope_tables(pos):
    half = ROPE_DIM // 2
    inv_freq = ROPE_THETA ** (-jnp.arange(half, dtype=F32) / half)
    ang = pos.astype(F32)[:, None] * inv_freq[None, :]
    z = jnp.zeros((pos.shape[0], LANES - ROPE_DIM), F32)
    cos, sin = jnp.cos(ang), jnp.sin(ang)
    return jnp.concatenate([cos, cos, z], axis=1), jnp.concatenate([sin, sin, z], axis=1)


def _swap_halves(w):
    half = ROPE_DIM // 2
    return jnp.concatenate([-w[..., half:], w[..., :half]], axis=-1)


def kernel(x_prompt, x_sample, mem_prompt, cache_mla_ckv, cache_mla_krope, cache_mem_k, cache_mem_v, state_rglru_h, state_rglru_conv, norm_mix, w_in, conv_w, conv_b, lru_wa, lru_ba, lru_wx, lru_bx, lru_lambda, mla_q_norm, mla_w_uq, mla_kv_norm, mla_w_uk, mla_w_uv, mem_norm, mem_w_k, mem_w_v, w_branch_rec, w_branch_mla, w_branch_mem, w_out, norm_ffn, router_group_w, router_group_b, router_expert_w, router_expert_b, moe_w_gate, moe_w_up, moe_w_down, norm_final):
    bp, sp, d = x_prompt.shape
    bs, ss, _ = x_sample.shape
    depth = w_in.shape[0]
    past = cache_mla_ckv.shape[2]
    m_tok = mem_prompt.shape[1]
    d_rnn = conv_w.shape[2]
    q_lora = mla_q_norm.shape[1]
    kv_lora = mla_kv_norm.shape[1]
    n_heads = mla_w_uq.shape[2]
    mem_heads = cache_mem_k.shape[3]
    mem_hd = cache_mem_k.shape[4]
    mem_q = mem_heads * mem_hd
    n_experts = moe_w_gate.shape[1]
    tp, tsm = bp * sp, bs * ss
    scale = (NOPE_DIM + ROPE_DIM) ** -0.5 * LOG2_E

    x = jnp.concatenate([x_prompt.reshape(tp, d), x_sample.reshape(tsm, d)], axis=0)
    pos_all = jnp.concatenate([jnp.tile(jnp.arange(sp), bp), jnp.tile(past + jnp.arange(ss), bs)])
    cos, sin = _rope_tables(pos_all)
    mem_flat = mem_prompt.reshape(bp * m_tok, d)
    conv_zero = jnp.zeros((bp, CONV_W - 1, d_rnn), F32)
    h_zero = jnp.zeros((bp, 1, d_rnn), F32)
    row2 = lambda v: v.reshape(1, -1)

    outs = {k: [] for k in ("p_ckv", "p_kr", "p_mk", "p_mv", "p_h", "p_conv", "s_ckv", "s_kr", "s_h", "s_conv")}
    for l in range(depth):
        o = 0
        segs = []
        for wdt in (d_rnn, d_rnn, q_lora, kv_lora, ROPE_DIM, mem_q, 3 * d):
            segs.append(w_in[l][:, o:o + wdt])
            o += wdt
        w_ux, w_uy, w_cq, w_ckv, w_kr, w_qm, w_g = segs
        w_xym = jnp.concatenate([w_ux, w_uy, w_qm], axis=1).astype(BF16)
        w_cq = w_cq.astype(BF16)
        w_kv = jnp.concatenate([w_ckv, w_kr, _swap_halves(w_kr)], axis=1).astype(BF16)
        w_g = w_g.astype(BF16)
        uq = mla_w_uq[l]
        uq_rope = uq[..., NOPE_DIM:]
        w_uq = jnp.concatenate([uq, _swap_halves(uq_rope)], axis=-1).reshape(q_lora, n_heads * HEAD_PAD).astype(BF16)
        w_uk = mla_w_uk[l].reshape(kv_lora, n_heads * NOPE_DIM).astype(BF16)
        w_uv = mla_w_uv[l].reshape(kv_lora, n_heads * V_DIM).astype(BF16)
        w_ukt = jnp.transpose(mla_w_uk[l], (1, 2, 0)).astype(BF16)
        w_uvh = jnp.transpose(mla_w_uv[l], (1, 0, 2)).astype(BF16)
        w_memkv = jnp.concatenate([mem_w_k[l], mem_w_v[l]], axis=1).astype(BF16)
        w_r32 = jnp.concatenate([router_group_w[l], router_expert_w[l],
                                 jnp.zeros((d, LANES - N_GROUPS - n_experts), F32)], axis=1)
        w_r = w_r32.astype(BF16)
        w_r_lo = (w_r32 - w_r.astype(F32)).astype(BF16)
        b_r = jnp.concatenate([router_group_b[l], router_expert_b[l],
                               jnp.zeros((LANES - N_GROUPS - n_experts,), F32)]).reshape(1, LANES)

        mkv, _ = norm_matmul(mem_flat, row2(mem_norm[l]), w_memkv, F32)
        mk = mkv[:, :mem_q].reshape(bp, m_tok, mem_heads, -1)
        mv = mkv[:, mem_q:].reshape(bp, m_tok, mem_heads, -1)
        outs["p_mk"].append(mk)
        outs["p_mv"].append(mv)

        z, xn = norm_matmul(x, row2(norm_mix[l]), w_xym, BF16)

        lru = (conv_w[l], row2(conv_b[l]), lru_wa[l].astype(BF16), lru_wx[l].astype(BF16),
               row2(lru_ba[l]), row2(lru_bx[l]), row2(lru_lambda[l]))
        rec_p, hl_p, cb_p = rglru(z, 0, bp, sp, conv_zero, h_zero, *lru, d_rnn)
        rec_s, hl_s, cb_s = rglru(z, tp, bs, ss, state_rglru_conv[l], state_rglru_h[l].reshape(bs, 1, d_rnn),
                                  *lru, d_rnn)
        outs["p_h"].append(hl_p.reshape(bp, d_rnn)); outs["p_conv"].append(cb_p)
        outs["s_h"].append(hl_s.reshape(bs, d_rnn)); outs["s_conv"].append(cb_s)

        q = q_proj(xn, w_cq, row2(mla_q_norm[l]), w_uq, cos, sin, n_heads, scale)
        ckv, kr, k, v = kv_proj(xn, w_kv, row2(mla_kv_norm[l]), w_uk, w_uv, cos, sin, n_heads)
        outs["p_ckv"].append(ckv[:tp].reshape(bp, sp, kv_lora)); outs["p_kr"].append(kr[:tp].reshape(bp, sp, ROPE_DIM))
        outs["s_ckv"].append(ckv[tp:].reshape(bs, ss, kv_lora)); outs["s_kr"].append(kr[tp:].reshape(bs, ss, ROPE_DIM))
        mla_p = attn_prompt(q, k, v, bp, sp)
        mla_s = attn_cached(q, ckv, kr, cache_mla_ckv, cache_mla_krope, l, w_ukt, w_uvh, tp, bs, ss)

        col_blk = (2 * d_rnn) // mem_q
        mkv4 = mkv.reshape(1, bp, m_tok, 2 * mem_q)
        mem_p = mem_attn(z, col_blk, mkv4, mkv4, 0, 0, bp, sp, mem_heads, mem_hd, v_head0=mem_heads)
        mem_s = mem_attn_cached(z, col_blk, cache_mem_k, cache_mem_v, l, tp, bs, ss)

        merged = merge_branches(xn, (rec_p, mla_p, mem_p), (rec_s, mla_s, mem_s), w_g,
                                w_branch_rec[l].astype(BF16),
                                w_branch_mla[l].astype(BF16), w_branch_mem[l].astype(BF16))
        x1, route = out_router(x, merged, w_out[l].astype(BF16), row2(norm_ffn[l]), w_r, w_r_lo, b_r, n_experts)
        x = hier_moe(x1, route, row2(norm_ffn[l]), moe_w_gate, moe_w_up, moe_w_down, l,
                     row2(norm_final), l == depth - 1, (tp, tsm) if l == depth - 1 else None)

    st = lambda name: jnp.stack(outs[name])
    y_prompt, y_sample = x
    return (y_prompt.reshape(bp, sp, d), y_sample.reshape(bs, ss, d),
            st("p_ckv"), st("p_kr"), st("p_mk"), st("p_mv"), st("p_h"), st("p_conv"),
            st("s_ckv"), st("s_kr"), st("s_h"), st("s_conv"))
```

```python
import functools

import jax
import jax.numpy as jnp
from jax import lax
from jax.experimental import pallas as pl
from jax.experimental.pallas import tpu as pltpu

F32 = jnp.float32
BF16 = jnp.bfloat16

EPS = 1e-6
CHUNK = 64
LRU_C = 8.0
ROPE_THETA = 10000.0
CONV_W = 4
NOPE_DIM = 128
ROPE_DIM = 64
V_DIM = 128
N_GROUPS = 4
TOP_K = 2

LANES = 128
SUBLANES = 8
HEAD_PAD = 2 * LANES
V7X_VMEM_BYTES = 64 * 1024 * 1024
VMEM_LIMIT = (V7X_VMEM_BYTES * 7) // 8
NEG_BIG = -1e30
LOG2_E = 1.4426950408889634


def _params(*sem):
    return pltpu.CompilerParams(dimension_semantics=sem, vmem_limit_bytes=VMEM_LIMIT)


def _tile(n, pref, mult=SUBLANES):
    t = min(pref, n)
    t -= t % mult
    while n % t:
        t -= mult
    return t


def _rms(x, g):
    y = x * lax.rsqrt(jnp.mean(x * x, axis=-1, keepdims=True) + EPS)
    return y * g


def _gelu(x):
    return 0.5 * x * (1.0 + jnp.tanh(0.7978845608028654 * (x + 0.044715 * (x * x * x))))


def _sigmoid(x):
    return 1.0 / (1.0 + jnp.exp(-x))


def _dot(a, b):
    return jnp.dot(a, b, preferred_element_type=F32)


def _dot_t(a, b):
    return lax.dot_general(a, b, (((1,), (1,)), ((), ())), preferred_element_type=F32)


def _norm_matmul_body(x_ref, g_ref, w_ref, z_ref, xn_ref, xn_scr):
    @pl.when(pl.program_id(1) == 0)
    def _():
        xn = _rms(x_ref[...], g_ref[...]).astype(BF16)
        xn_scr[...] = xn
        xn_ref[...] = xn

    z_ref[...] = _dot(xn_scr[...], w_ref[...]).astype(z_ref.dtype)


def norm_matmul(x, g, w, out_dtype, tm_pref=768, tn_pref=1024):
    t, d = x.shape
    n = w.shape[1]
    tm, tn = _tile(t, tm_pref, 16), _tile(n, tn_pref, LANES)
    return pl.pallas_call(
        _norm_matmul_body,
        grid=(t // tm, n // tn),
        in_specs=[
            pl.BlockSpec((tm, d), lambda i, j: (i, 0)),
            pl.BlockSpec((1, d), lambda i, j: (0, 0)),
            pl.BlockSpec((d, tn), lambda i, j: (0, j)),
        ],
        out_specs=[
            pl.BlockSpec((tm, tn), lambda i, j: (i, j)),
            pl.BlockSpec((tm, d), lambda i, j: (i, 0)),
        ],
        out_shape=[jax.ShapeDtypeStruct((t, n), out_dtype), jax.ShapeDtypeStruct((t, d), BF16)],
        scratch_shapes=[pltpu.VMEM((tm, d), BF16)],
        compiler_params=_params("arbitrary", "arbitrary"),
        name="norm_matmul",
    )(x, g, w)


def _rope_mix(r, c, s):
    return r * c + pltpu.roll(r, ROPE_DIM, 1) * s


def _q_proj_body(xn_ref, wq_ref, gq_ref, wuq_ref, cos_ref, sin_ref, q_ref, *, n_heads, scale):
    cq = _dot(xn_ref[...], wq_ref[...])
    cqn = _rms(cq, gq_ref[...]).astype(BF16)
    c = cos_ref[...] * scale
    s = sin_ref[...] * scale
    for h in range(n_heads):
        qh = _dot(cqn, wuq_ref[:, h * HEAD_PAD:(h + 1) * HEAD_PAD])
        q_ref[h, :, :NOPE_DIM] = (qh[:, :NOPE_DIM] * scale).astype(BF16)
        q_ref[h, :, NOPE_DIM:] = _rope_mix(qh[:, NOPE_DIM:], c, s).astype(BF16)


def q_proj(xn, wq, gq, wuq, cos, sin, n_heads, scale, tm_pref=512):
    t, d = xn.shape
    ql = wq.shape[1]
    tm = _tile(t, tm_pref, 16)
    return pl.pallas_call(
        functools.partial(_q_proj_body, n_heads=n_heads, scale=scale),
        grid=(t // tm,),
        in_specs=[
            pl.BlockSpec((tm, d), lambda i: (i, 0)),
            pl.BlockSpec((d, ql), lambda i: (0, 0)),
            pl.BlockSpec((1, ql), lambda i: (0, 0)),
            pl.BlockSpec((ql, n_heads * HEAD_PAD), lambda i: (0, 0)),
            pl.BlockSpec((tm, LANES), lambda i: (i, 0)),
            pl.BlockSpec((tm, LANES), lambda i: (i, 0)),
        ],
        out_specs=pl.BlockSpec((n_heads, tm, HEAD_PAD), lambda i: (0, i, 0)),
        out_shape=jax.ShapeDtypeStruct((n_heads, t, HEAD_PAD), BF16),
        compiler_params=_params("arbitrary"),
        name="q_proj",
    )(xn, wq, gq, wuq, cos, sin)


def _kv_proj_body(xn_ref, wkv_ref, gkv_ref, wuk_ref, wuv_ref, cos_ref, sin_ref,
                  ckv_ref, kr_ref, k_ref, v_ref, *, n_heads, kv_lora):
    ckr = _dot(xn_ref[...], wkv_ref[...])
    ckv = _rms(ckr[:, :kv_lora], gkv_ref[...])
    ckv_ref[...] = ckv
    rr = _rope_mix(ckr[:, kv_lora:], cos_ref[...], sin_ref[...])
    kr_ref[...] = rr[:, :ROPE_DIM]
    ckv_b = ckv.astype(BF16)
    rr_b = rr.astype(BF16)
    kn = _dot(ckv_b, wuk_ref[...])
    vv = _dot(ckv_b, wuv_ref[...])
    for h in range(n_heads):
        k_ref[h, :, :NOPE_DIM] = kn[:, h * NOPE_DIM:(h + 1) * NOPE_DIM].astype(BF16)
        k_ref[h, :, NOPE_DIM:] = rr_b
        v_ref[h] = vv[:, h * V_DIM:(h + 1) * V_DIM].astype(BF16)


def kv_proj(xn, wkv, gkv, wuk, wuv, cos, sin, n_heads, tm_pref=512):
    t, d = xn.shape
    kv_lora = gkv.shape[1]
    tm = _tile(t, tm_pref, 16)
    return pl.pallas_call(
        functools.partial(_kv_proj_body, n_heads=n_heads, kv_lora=kv_lora),
        grid=(t // tm,),
        in_specs=[
            pl.BlockSpec((tm, d), lambda i: (i, 0)),
            pl.BlockSpec((d, kv_lora + LANES), lambda i: (0, 0)),
            pl.BlockSpec((1, kv_lora), lambda i: (0, 0)),
            pl.BlockSpec((kv_lora, n_heads * NOPE_DIM), lambda i: (0, 0)),
            pl.BlockSpec((kv_lora, n_heads * V_DIM), lambda i: (0, 0)),
            pl.BlockSpec((tm, LANES), lambda i: (i, 0)),
            pl.BlockSpec((tm, LANES), lambda i: (i, 0)),
        ],
        out_specs=[
            pl.BlockSpec((tm, kv_lora), lambda i: (i, 0)),
            pl.BlockSpec((tm, ROPE_DIM), lambda i: (i, 0)),
            pl.BlockSpec((n_heads, tm, HEAD_PAD), lambda i: (0, i, 0)),
            pl.BlockSpec((n_heads, tm, V_DIM), lambda i: (0, i, 0)),
        ],
        out_shape=[
            jax.ShapeDtypeStruct((t, kv_lora), F32),
            jax.ShapeDtypeStruct((t, ROPE_DIM), F32),
            jax.ShapeDtypeStruct((n_heads, t, HEAD_PAD), BF16),
            jax.ShapeDtypeStruct((n_heads, t, V_DIM), BF16),
        ],
        compiler_params=_params("arbitrary"),
        name="kv_proj",
    )(xn, wkv, gkv, wuk, wuv, cos, sin)


def _rglru_body(ux_ref, uy_ref, conv0_ref, h0_ref, cw_ref, cb_ref, wa_ref, wx_ref, ba_ref, bx_ref,
                lam_ref, o_ref, hlast_ref, nconv_ref, ubuf, hcar, *, ts, n_blocks):
    hist = SUBLANES
    col = lambda c: slice(c * LANES, (c + 1) * LANES)
    row_id = lax.broadcasted_iota(jnp.int32, (ts, LANES), 0)

    @pl.when(pl.program_id(1) == 0)
    def _():
        ubuf[hist - (CONV_W - 1):hist, :] = conv0_ref[0]
        hcar[...] = h0_ref[0]

    ubuf[hist:hist + ts, :] = ux_ref[...].astype(F32)
    cw = cw_ref[...]
    xc = cb_ref[...]
    for k in range(CONV_W):
        off = hist - (CONV_W - 1) + k
        xc = xc + ubuf[off:off + ts, :] * cw[k:k + 1, :]
    tail = ubuf[hist + ts - (CONV_W - 1):hist + ts, :]
    nconv_ref[0] = tail
    ubuf[hist - (CONV_W - 1):hist, :] = tail

    lam = lam_ref[...]
    sp = jnp.maximum(-lam, 0.0) + jnp.log(1.0 + jnp.exp(-jnp.abs(lam)))
    for n in range(n_blocks):
        sl = col(n)
        xcn = xc[:, sl]
        xb = xcn.astype(BF16)
        r = _sigmoid(_dot(xb, wa_ref[n]) + ba_ref[:, sl])
        i = _sigmoid(_dot(xb, wx_ref[n]) + bx_ref[:, sl])
        a = jnp.exp((-LRU_C) * r * sp[:, sl])
        b = jnp.sqrt(1.0 - a * a) * (i * xcn)
        shift = 1
        while shift < ts:
            if shift < SUBLANES:
                keep = row_id >= shift
                b = b + a * jnp.where(keep, pltpu.roll(b, shift, 0), 0.0)
                a = a * jnp.where(keep, pltpu.roll(a, shift, 0), 1.0)
            else:
                b = jnp.concatenate([b[:shift], b[shift:] + a[shift:] * b[:-shift]], axis=0)
                a = jnp.concatenate([a[:shift], a[shift:] * a[:-shift]], axis=0)
            shift *= 2
        hfin = b + a * hcar[:, sl]
        hcar[:, sl] = hfin[ts - 1:ts, :]
        o_ref[:, sl] = (hfin * _gelu(uy_ref[:, sl].astype(F32))).astype(o_ref.dtype)
    hlast_ref[0] = hcar[...]


def rglru(z, row0, n_batch, seq, conv0, h0, cw, cb, wa, wx, ba, bx, lam, d_rnn, ts_pref=256):
    ts = _tile(seq, ts_pref, 16)
    nst = seq // ts
    blk0 = row0 // ts
    n_blocks = wa.shape[0]
    assert d_rnn == n_blocks * LANES and ts % SUBLANES == 0
    vec = pl.BlockSpec((1, d_rnn), lambda b, s: (0, 0))
    in_specs = [
        pl.BlockSpec((ts, d_rnn), lambda b, s: (blk0 + b * nst + s, 0)),
        pl.BlockSpec((ts, d_rnn), lambda b, s: (blk0 + b * nst + s, 1)),
        pl.BlockSpec((1, CONV_W - 1, d_rnn), lambda b, s: (b, 0, 0)),
        pl.BlockSpec((1, 1, d_rnn), lambda b, s: (b, 0, 0)),
        pl.BlockSpec((CONV_W, d_rnn), lambda b, s: (0, 0)),
        vec,
        pl.BlockSpec(wa.shape, lambda b, s: (0, 0, 0)),
        pl.BlockSpec(wx.shape, lambda b, s: (0, 0, 0)),
        vec, vec, vec,
    ]
    return pl.pallas_call(
        functools.partial(_rglru_body, ts=ts, n_blocks=n_blocks),
        grid=(n_batch, nst),
        in_specs=in_specs,
        out_specs=[
            pl.BlockSpec((ts, d_rnn), lambda b, s: (b * nst + s, 0)),
            pl.BlockSpec((1, 1, d_rnn), lambda b, s: (b, 0, 0)),
            pl.BlockSpec((1, CONV_W - 1, d_rnn), lambda b, s: (b, 0, 0)),
        ],
        out_shape=[
            jax.ShapeDtypeStruct((n_batch * seq, d_rnn), BF16),
            jax.ShapeDtypeStruct((n_batch, 1, d_rnn), F32),
            jax.ShapeDtypeStruct((n_batch, CONV_W - 1, d_rnn), F32),
        ],
        scratch_shapes=[
            pltpu.VMEM((SUBLANES + ts, d_rnn), F32),
            pltpu.VMEM((1, d_rnn), F32),
        ],
        compiler_params=_params("arbitrary", "arbitrary"),
        name="rglru",
    )(z, z, conv0, h0, cw, cb, wa, wx, ba, bx, lam)


def _softmax_step(s, v, m, l, acc):
    m_new = jnp.maximum(m, jnp.max(s, axis=-1, keepdims=True))
    alpha = jnp.exp2(m - m_new)
    p = jnp.exp2(s - m_new)
    l = alpha * l + jnp.sum(p, axis=-1, keepdims=True)
    acc = alpha * acc + _dot(p.astype(BF16), v)
    return m_new, l, acc


def _attn_prompt_body(q_ref, k_ref, v_ref, o_ref, *, tq, tk, hpb):
    qi = pl.program_id(2)
    qs = [q_ref[h] for h in range(hpb)]

    def kv(h, j):
        off = pl.multiple_of(j * tk, tk)
        return k_ref[h, pl.ds(off, tk), :], v_ref[h, pl.ds(off, tk), :]

    def body(j, carry):
        out = []
        for h in range(hpb):
            k, v = kv(h, j)
            out.append(_softmax_step(_dot_t(qs[h], k), v, *carry[h]))
        return tuple(out)

    init = (jnp.full((tq, 1), NEG_BIG, F32), jnp.zeros((tq, 1), F32), jnp.zeros((tq, V_DIM), F32))
    n_full = (qi * tq) // tk
    carry = lax.fori_loop(0, n_full, body, (init,) * hpb)
    shift = CHUNK.bit_length() - 1
    off = pl.multiple_of(n_full * tk, tq)
    q_off = qi * tq - off

    def boundary(width):
        row_chunk = (lax.broadcasted_iota(jnp.int32, (tq, width), 0) + q_off) >> shift
        col_chunk = lax.broadcasted_iota(jnp.int32, (tq, width), 1) >> shift
        visible = col_chunk <= row_chunk
        for h in range(hpb):
            k = k_ref[h, pl.ds(off, width), :]
            v = v_ref[h, pl.ds(off, width), :]
            s = jnp.where(visible, _dot_t(qs[h], k), NEG_BIG)
            m, l, acc = _softmax_step(s, v, *carry[h])
            o_ref[:, h * V_DIM:(h + 1) * V_DIM] = (acc / l).astype(o_ref.dtype)

    if tk == tq:
        boundary(tq)
    else:
        @pl.when(q_off == 0)
        def _():
            boundary(tq)

        @pl.when(q_off != 0)
        def _():
            boundary(tk)


def attn_prompt(q, k, v, n_batch, seq, tq_pref=512, tk_pref=1024, hpb=2):
    n_heads = q.shape[0]
    assert n_heads % hpb == 0
    tq = _tile(seq, tq_pref, CHUNK)
    tk = _tile(seq, tk_pref, tq)
    nq = seq // tq
    return pl.pallas_call(
        functools.partial(_attn_prompt_body, tq=tq, tk=tk, hpb=hpb),
        grid=(n_batch, n_heads // hpb, nq),
        in_specs=[
            pl.BlockSpec((hpb, tq, HEAD_PAD), lambda b, h, i: (h, b * nq + i, 0)),
            pl.BlockSpec((hpb, seq, HEAD_PAD), lambda b, h, i: (h, b, 0)),
            pl.BlockSpec((hpb, seq, V_DIM), lambda b, h, i: (h, b, 0)),
        ],
        out_specs=pl.BlockSpec((tq, hpb * V_DIM), lambda b, h, i: (b * nq + i, h)),
        out_shape=jax.ShapeDtypeStruct((n_batch * seq, n_heads * V_DIM), BF16),
        compiler_params=_params("arbitrary", "arbitrary", "arbitrary"),
        name="attn_prompt",
    )(q, k, v)


def _attn_cached_body(q_ref, ckvn_ref, krn_ref, cckv_ref, ckr_ref, wukt_ref, wuv_ref, o_ref,
                      ql_scr, qr_scr, *, n_heads, sq, tk, past):
    for h in range(n_heads):
        qh = q_ref[h]
        ql_scr[h * sq:(h + 1) * sq, :] = _dot(qh[:, :NOPE_DIM], wukt_ref[h]).astype(BF16)
        qr_scr[h * sq:(h + 1) * sq, :] = qh[:, NOPE_DIM:NOPE_DIM + ROPE_DIM]
    ql = ql_scr[...]
    qr = qr_scr[...]
    rows = n_heads * sq
    kv_lora = ql.shape[1]

    def body(j, carry):
        off = pl.multiple_of(j * tk, tk)
        kc = cckv_ref[0, 0, pl.ds(off, tk), :].astype(BF16)
        kr = ckr_ref[0, 0, pl.ds(off, tk), :].astype(BF16)
        return _softmax_step(_dot_t(ql, kc) + _dot_t(qr, kr), kc, *carry)

    init = (jnp.full((rows, 1), NEG_BIG, F32), jnp.zeros((rows, 1), F32), jnp.zeros((rows, kv_lora), F32))
    carry = lax.fori_loop(0, past // tk, body, init)
    kc = ckvn_ref[...].astype(BF16)
    kr = krn_ref[...].astype(BF16)
    m, l, acc = _softmax_step(_dot_t(ql, kc) + _dot_t(qr, kr), kc, *carry)
    o_lat = (acc / l).astype(BF16)
    for h in range(n_heads):
        o_ref[:, h * V_DIM:(h + 1) * V_DIM] = _dot(o_lat[h * sq:(h + 1) * sq, :], wuv_ref[h]).astype(o_ref.dtype)


def attn_cached(q, ckv, kr, cache_ckv, cache_kr, layer, wukt, wuv, row0, n_batch, sq, tk_pref=1024):
    n_heads = q.shape[0]
    kv_lora = ckv.shape[1]
    past = cache_ckv.shape[2]
    tk = _tile(past, tk_pref, 16)
    blk0 = row0 // sq
    return pl.pallas_call(
        functools.partial(_attn_cached_body, n_heads=n_heads, sq=sq, tk=tk, past=past),
        grid=(n_batch,),
        in_specs=[
            pl.BlockSpec((n_heads, sq, HEAD_PAD), lambda b: (0, blk0 + b, 0)),
            pl.BlockSpec((sq, kv_lora), lambda b: (blk0 + b, 0)),
            pl.BlockSpec((sq, ROPE_DIM), lambda b: (blk0 + b, 0)),
            pl.BlockSpec((1, 1, past, kv_lora), lambda b: (layer, b, 0, 0)),
            pl.BlockSpec((1, 1, past, ROPE_DIM), lambda b: (layer, b, 0, 0)),
            pl.BlockSpec(wukt.shape, lambda b: (0, 0, 0)),
            pl.BlockSpec(wuv.shape, lambda b: (0, 0, 0)),
        ],
        out_specs=pl.BlockSpec((sq, n_heads * V_DIM), lambda b: (b, 0)),
        out_shape=jax.ShapeDtypeStruct((n_batch * sq, n_heads * V_DIM), BF16),
        scratch_shapes=[pltpu.VMEM((n_heads * sq, kv_lora), BF16), pltpu.VMEM((n_heads * sq, ROPE_DIM), BF16)],
        compiler_params=_params("arbitrary"),
        name="attn_cached",
    )(q, ckv, kr, cache_ckv, cache_kr, wukt, wuv)


def _mem_attn_body(q_ref, *refs, n_heads):
    k_refs, v_refs, o_ref = refs[:n_heads], refs[n_heads:2 * n_heads], refs[2 * n_heads]
    hd = q_ref.shape[1] // n_heads
    scale = hd ** -0.5
    for h in range(n_heads):
        sl = slice(h * hd, (h + 1) * hd)
        s = _dot_t(q_ref[:, sl], k_refs[h][...].astype(BF16)) * scale
        e = jnp.exp(s - jnp.max(s, axis=-1, keepdims=True))
        o = _dot(e.astype(BF16), v_refs[h][...].astype(BF16)) / jnp.sum(e, axis=-1, keepdims=True)
        o_ref[:, sl] = o.astype(o_ref.dtype)


def mem_attn(z, col_blk, mem_k, mem_v, layer, row0, n_batch, seq, n_heads, hd, v_head0=0, tm_pref=512):
    m_tok = mem_k.shape[2]
    dq = n_heads * hd
    tm = _tile(seq, tm_pref, 16)
    nst = seq // tm
    blk0 = row0 // tm

    def head_spec(h):
        return pl.BlockSpec((None, None, m_tok, hd), lambda b, s: (layer, b, 0, h))

    k_specs = [head_spec(h) for h in range(n_heads)]
    v_specs = [head_spec(v_head0 + h) for h in range(n_heads)]
    return pl.pallas_call(
        functools.partial(_mem_attn_body, n_heads=n_heads),
        grid=(n_batch, nst),
        in_specs=[pl.BlockSpec((tm, dq), lambda b, s: (blk0 + b * nst + s, col_blk))] + k_specs + v_specs,
        out_specs=pl.BlockSpec((tm, dq), lambda b, s: (b * nst + s, 0)),
        out_shape=jax.ShapeDtypeStruct((n_batch * seq, dq), BF16),
        compiler_params=_params("arbitrary", "arbitrary"),
        name="mem_attn",
    )(z, *([mem_k] * n_heads), *([mem_v] * n_heads))


def _mem_attn_cached_body(q_ref, k_hbm, v_hbm, o_ref, kbuf, vbuf, sem, *, layer, n_heads):
    b = pl.program_id(0)
    n = pl.num_programs(0)

    def copies(batch, slot):
        out = []
        for h in range(n_heads):
            out.append(pltpu.make_async_copy(k_hbm.at[layer, batch, :, h, :], kbuf.at[slot, h], sem.at[slot]))
            out.append(pltpu.make_async_copy(v_hbm.at[layer, batch, :, h, :], vbuf.at[slot, h], sem.at[slot]))
        return out

    @pl.when(b == 0)
    def _():
        for c in copies(0, 0):
            c.start()

    @pl.when(b + 1 < n)
    def _():
        for c in copies(b + 1, (b + 1) % 2):
            c.start()

    slot = b % 2
    for c in copies(b, slot):
        c.wait()
    hd = kbuf.shape[3]
    scale = hd ** -0.5
    for h in range(n_heads):
        sl = slice(h * hd, (h + 1) * hd)
        s = _dot_t(q_ref[:, sl], kbuf[slot, h].astype(BF16)) * scale
        e = jnp.exp(s - jnp.max(s, axis=-1, keepdims=True))
        o = _dot(e.astype(BF16), vbuf[slot, h].astype(BF16)) / jnp.sum(e, axis=-1, keepdims=True)
        o_ref[:, sl] = o.astype(o_ref.dtype)


def mem_attn_cached(z, col_blk, mem_k, mem_v, layer, row0, n_batch, seq):
    _, _, m_tok, n_heads, hd = mem_k.shape
    dq = n_heads * hd
    blk0 = row0 // seq
    return pl.pallas_call(
        functools.partial(_mem_attn_cached_body, layer=layer, n_heads=n_heads),
        grid=(n_batch,),
        in_specs=[
            pl.BlockSpec((seq, dq), lambda b: (blk0 + b, col_blk)),
            pl.BlockSpec(memory_space=pl.ANY),
            pl.BlockSpec(memory_space=pl.ANY),
        ],
        out_specs=pl.BlockSpec((seq, dq), lambda b: (b, 0)),
        out_shape=jax.ShapeDtypeStruct((n_batch * seq, dq), BF16),
        scratch_shapes=[pltpu.VMEM((2, n_heads, m_tok, hd), F32), pltpu.VMEM((2, n_heads, m_tok, hd), F32),
                        pltpu.SemaphoreType.DMA((2,))],
        compiler_params=_params("arbitrary"),
        name="mem_attn_cached",
    )(z, mem_k, mem_v)


def _merge_body(xn_ref, p0_ref, p1_ref, p2_ref, s0_ref, s1_ref, s2_ref,
                g0_ref, g1_ref, g2_ref, w0_ref, w1_ref, w2_ref, o_ref, *, n_first):
    def emit(a0_ref, a1_ref, a2_ref):
        xn = xn_ref[...]
        acc = _sigmoid(_dot(xn, g0_ref[...])) * _dot(a0_ref[...], w0_ref[...])
        acc = acc + _sigmoid(_dot(xn, g1_ref[...])) * _dot(a1_ref[...], w1_ref[...])
        acc = acc + _sigmoid(_dot(xn, g2_ref[...])) * _dot(a2_ref[...], w2_ref[...])
        o_ref[...] = acc.astype(o_ref.dtype)

    i = pl.program_id(1)

    @pl.when(i < n_first)
    def _():
        emit(p0_ref, p1_ref, p2_ref)

    @pl.when(i >= n_first)
    def _():
        emit(s0_ref, s1_ref, s2_ref)


def merge_branches(xn, first, second, wg, w0, w1, w2, tm_pref=256, tn_pref=512):
    t, d = xn.shape
    t1, t2 = first[0].shape[0], second[0].shape[0]
    tm, tn = _tile(min(t1, t2), tm_pref, 16), _tile(d, tn_pref, LANES)
    assert t1 % tm == 0 and t2 % tm == 0 and t1 + t2 == t
    n1 = t1 // tm
    nj = d // tn
    act = pl.BlockSpec((tm, d), lambda j, i: (i, 0))
    act1 = pl.BlockSpec((tm, d), lambda j, i: (jnp.minimum(i, n1 - 1), 0))
    act2 = pl.BlockSpec((tm, d), lambda j, i: (jnp.maximum(i - n1, 0), 0))
    wsp = pl.BlockSpec((d, tn), lambda j, i: (0, j))
    return pl.pallas_call(
        functools.partial(_merge_body, n_first=n1),
        grid=(nj, t // tm),
        in_specs=[act, act1, act1, act1, act2, act2, act2,
                  pl.BlockSpec((d, tn), lambda j, i: (0, j)),
                  pl.BlockSpec((d, tn), lambda j, i: (0, nj + j)),
                  pl.BlockSpec((d, tn), lambda j, i: (0, 2 * nj + j)),
                  wsp, wsp, wsp],
        out_specs=pl.BlockSpec((tm, tn), lambda j, i: (i, j)),
        out_shape=jax.ShapeDtypeStruct((t, d), BF16),
        compiler_params=_params("arbitrary", "arbitrary"),
        name="merge_branches",
    )(xn, *first, *second, wg, wg, wg, w0, w1, w2)


def _out_router_body(x_ref, m_ref, wo_ref, gf_ref, wr_ref, wrl_ref, br_ref, x1_ref, route_ref, *, n_experts):
    x1 = x_ref[...] + _dot(m_ref[...], wo_ref[...])
    x1_ref[...] = x1
    hn = _rms(x1, gf_ref[...])
    hn_hi = hn.astype(BF16)
    hn_lo = (hn - hn_hi.astype(F32)).astype(BF16)
    lg = (_dot(hn_hi, wr_ref[...]) + (_dot(hn_lo, wr_ref[...]) + _dot(hn_hi, wrl_ref[...]))) + br_ref[...]
    lane = lax.broadcasted_iota(jnp.int32, lg.shape, 1).astype(F32)
    epg = n_experts // N_GROUPS
    is_g = lane < N_GROUPS
    gl = jnp.where(is_g, lg, -jnp.inf)
    gmax = jnp.max(gl, axis=-1, keepdims=True)
    g_sel = jnp.min(jnp.where(gl == gmax, lane, LANES), axis=-1, keepdims=True)
    pg = 1.0 / jnp.sum(jnp.where(is_g, jnp.exp(lg - gmax), 0.0), axis=-1, keepdims=True)
    lo = N_GROUPS + g_sel * epg
    el = jnp.where((lane >= lo) & (lane < lo + epg), lg, -jnp.inf)
    t1 = jnp.max(el, axis=-1, keepdims=True)
    i1 = jnp.min(jnp.where(el == t1, lane, LANES), axis=-1, keepdims=True)
    el2 = jnp.where(lane == i1, -jnp.inf, el)
    t2 = jnp.max(el2, axis=-1, keepdims=True)
    i2 = jnp.min(jnp.where(el2 == t2, lane, LANES), axis=-1, keepdims=True)
    dlt = jnp.exp(t2 - t1)
    w1 = pg / (1.0 + dlt)
    w2 = pg * dlt / (1.0 + dlt)
    e1 = i1 - N_GROUPS
    e2 = i2 - N_GROUPS
    route = jnp.where(lane == 0, e1, jnp.where(lane == 1, e2, jnp.where(lane == 2, w1, jnp.where(lane == 3, w2, 0.0))))
    route_ref[...] = route


def out_router(x, merged, wo, gf, wr, wr_lo, br, n_experts, tm_pref=512):
    t, d = x.shape
    tm = _tile(t, tm_pref, 16)
    row = pl.BlockSpec((tm, d), lambda i: (i, 0))
    return pl.pallas_call(
        functools.partial(_out_router_body, n_experts=n_experts),
        grid=(t // tm,),
        in_specs=[row, row,
                  pl.BlockSpec((d, d), lambda i: (0, 0)),
                  pl.BlockSpec((1, d), lambda i: (0, 0)),
                  pl.BlockSpec((d, LANES), lambda i: (0, 0)),
                  pl.BlockSpec((d, LANES), lambda i: (0, 0)),
                  pl.BlockSpec((1, LANES), lambda i: (0, 0))],
        out_specs=[row, pl.BlockSpec((tm, LANES), lambda i: (i, 0))],
        out_shape=[jax.ShapeDtypeStruct((t, d), F32), jax.ShapeDtypeStruct((t, LANES), F32)],
        compiler_params=_params("arbitrary"),
        name="out_router",
    )(x, merged, wo, gf, wr, wr_lo, br)


def _gmm_body(src_ref, te_ref, tv_ref, x_hbm, gf_ref, wg_ref, wu_ref, wd_ref, y_ref,
              buf, sem, wg_b, wu_b, wd_b, *, tm):
    i = pl.program_id(0)
    n = pl.num_programs(0)

    def fetch(tile, slot):
        def issue(j, carry):
            base = pl.multiple_of(j * SUBLANES, SUBLANES)
            for k in range(SUBLANES):
                row = src_ref[tile * tm + base + k]
                pltpu.make_async_copy(x_hbm.at[pl.ds(row, 1)], buf.at[slot, j, pl.ds(k, 1)],
                                      sem.at[slot]).start()
            return carry

        lax.fori_loop(0, tm // SUBLANES, issue, 0)

    @pl.when(jnp.logical_and(i == 0, tv_ref[0] > 0))
    def _():
        fetch(0, 0)

    @pl.when(jnp.logical_and(i + 1 < n, tv_ref[jnp.minimum(i + 1, n - 1)] > 0))
    def _():
        fetch(i + 1, (i + 1) % 2)

    valid = tv_ref[i] > 0
    new_expert = jnp.logical_or(i == 0, te_ref[i] != te_ref[jnp.maximum(i - 1, 0)])

    @pl.when(jnp.logical_and(valid, new_expert))
    def _():
        wg_b[...] = wg_ref[0, 0].astype(BF16)
        wu_b[...] = wu_ref[0, 0].astype(BF16)
        wd_b[...] = wd_ref[0, 0].astype(BF16)

    @pl.when(valid)
    def _():
        slot = i % 2
        for j in range(tm // SUBLANES):
            pltpu.make_async_copy(x_hbm.at[pl.ds(0, SUBLANES)], buf.at[slot, j], sem.at[slot]).wait()
        hn = _rms(buf[slot].reshape(tm, buf.shape[3]), gf_ref[...]).astype(BF16)
        a = _dot(hn, wg_b[...])
        act = (a * _sigmoid(a) * _dot(hn, wu_b[...])).astype(BF16)
        y_ref[...] = _dot(act, wd_b[...])

    @pl.when(jnp.logical_not(valid))
    def _():
        y_ref[...] = jnp.zeros_like(y_ref)


def grouped_mlp(x, src, gf, w_gate, w_up, w_down, layer, tile_expert, tile_valid, tm):
    d = x.shape[1]
    p = src.shape[0]
    d_expert = w_down.shape[2]
    w_in_spec = pl.BlockSpec((1, 1, d, d_expert), lambda i, src, te, tv: (layer, te[i], 0, 0))
    return pl.pallas_call(
        functools.partial(_gmm_body, tm=tm),
        grid_spec=pltpu.PrefetchScalarGridSpec(
            num_scalar_prefetch=3,
            grid=(p // tm,),
            in_specs=[
                pl.BlockSpec(memory_space=pl.ANY),
                pl.BlockSpec((1, d), lambda i, src, te, tv: (0, 0)),
                w_in_spec,
                w_in_spec,
                pl.BlockSpec((1, 1, d_expert, d), lambda i, src, te, tv: (layer, te[i], 0, 0)),
            ],
            out_specs=pl.BlockSpec((tm, d), lambda i, src, te, tv: (i, 0)),
            scratch_shapes=[pltpu.VMEM((2, tm // SUBLANES, SUBLANES, d), F32), pltpu.SemaphoreType.DMA((2,)),
                            pltpu.VMEM((d, d_expert), BF16), pltpu.VMEM((d, d_expert), BF16),
                            pltpu.VMEM((d_expert, d), BF16)],
        ),
        out_shape=jax.ShapeDtypeStruct((p, d), F32),
        compiler_params=_params("arbitrary"),
        name="moe_grouped_mlp",
    )(src, tile_expert, tile_valid, x, gf, w_gate, w_up, w_down)


def _combine_body(pos_ref, x1_ref, route_ref, gn_ref, y_hbm, *rest, tm, final_norm, n_first):
    outs, (buf, sem) = rest[:-2], rest[-2:]
    i = pl.program_id(0)
    n = pl.num_programs(0)

    groups = tm // SUBLANES

    def fetch(tile, slot):
        def issue(j, carry):
            for s in range(SUBLANES):
                for k in range(TOP_K):
                    src_row = pos_ref[(tile * tm + j * SUBLANES + s) * TOP_K + k]
                    pltpu.make_async_copy(y_hbm.at[pl.ds(src_row, 1)], buf.at[slot, k * groups + j, pl.ds(s, 1)],
                                          sem.at[slot]).start()
            return carry

        lax.fori_loop(0, groups, issue, 0)

    @pl.when(i == 0)
    def _():
        fetch(0, 0)

    @pl.when(i + 1 < n)
    def _():
        fetch(i + 1, (i + 1) % 2)

    slot = i % 2
    for g in range(TOP_K * groups):
        pltpu.make_async_copy(y_hbm.at[pl.ds(0, SUBLANES)], buf.at[slot, g], sem.at[slot]).wait()
    rt = route_ref[...]
    d = buf.shape[3]
    y_a = buf[slot, 0:groups].reshape(tm, d)
    y_b = buf[slot, groups:TOP_K * groups].reshape(tm, d)
    x2 = x1_ref[...] + (rt[:, 2:3] * y_a + rt[:, 3:4] * y_b)
    if final_norm:
        x2 = _rms(x2, gn_ref[...])
    if n_first is None:
        outs[0][...] = x2
    else:
        @pl.when(i < n_first)
        def _():
            outs[0][...] = x2

        @pl.when(i >= n_first)
        def _():
            outs[1][...] = x2


def combine(x1, route, y, pos, gn, final_norm, split=None, tm_pref=256):
    t, d = x1.shape
    tm = _tile(t if split is None else min(split), tm_pref, SUBLANES)
    if split is None:
        n_first = None
        out_specs = pl.BlockSpec((tm, d), lambda i, pos: (i, 0))
        out_shape = jax.ShapeDtypeStruct((t, d), F32)
    else:
        assert split[0] % tm == 0 and split[1] % tm == 0 and sum(split) == t
        n_first = split[0] // tm
        out_specs = [pl.BlockSpec((tm, d), lambda i, pos: (jnp.minimum(i, n_first - 1), 0)),
                     pl.BlockSpec((tm, d), lambda i, pos: (jnp.maximum(i - n_first, 0), 0))]
        out_shape = [jax.ShapeDtypeStruct((split[0], d), F32), jax.ShapeDtypeStruct((split[1], d), F32)]
    return pl.pallas_call(
        functools.partial(_combine_body, tm=tm, final_norm=final_norm, n_first=n_first),
        grid_spec=pltpu.PrefetchScalarGridSpec(
            num_scalar_prefetch=1,
            grid=(t // tm,),
            in_specs=[
                pl.BlockSpec((tm, d), lambda i, pos: (i, 0)),
                pl.BlockSpec((tm, LANES), lambda i, pos: (i, 0)),
                pl.BlockSpec((1, d), lambda i, pos: (0, 0)),
                pl.BlockSpec(memory_space=pl.ANY),
            ],
            out_specs=out_specs,
            scratch_shapes=[pltpu.VMEM((2, TOP_K * tm // SUBLANES, SUBLANES, d), F32),
                            pltpu.SemaphoreType.DMA((2,))],
        ),
        out_shape=out_shape,
        compiler_params=_params("arbitrary"),
        name="moe_combine",
    )(pos, x1, route, gn, y)


def _dispatch_plan(route, n_experts, tm):
    t = route.shape[0]
    n_assign = TOP_K * t
    eid = route[:, :TOP_K].astype(jnp.int32).reshape(n_assign)
    onehot = eid[:, None] == jnp.arange(n_experts, dtype=jnp.int32)[None, :]
    blk = _tile(n_assign, LANES, 1)
    n_blk = n_assign // blk
    oh = onehot.astype(BF16).reshape(n_blk, blk, n_experts)
    below = (jnp.arange(blk)[:, None] > jnp.arange(blk)[None, :]).astype(BF16)
    within = jnp.einsum("ij,bje->bie", below, oh, preferred_element_type=F32)
    blk_sum = jnp.sum(oh.astype(F32), axis=1)
    blk_off = jnp.cumsum(blk_sum, axis=0) - blk_sum
    rank_all = (within + blk_off[:, None, :]).reshape(n_assign, n_experts)
    counts = jnp.sum(blk_sum, axis=0).astype(jnp.int32)
    padded = ((counts + tm - 1) // tm) * tm
    ends = jnp.cumsum(padded)
    starts = ends - padded
    pos = jnp.sum(jnp.where(onehot, rank_all + starts.astype(F32)[None, :], 0.0), axis=1).astype(jnp.int32)
    n_tiles = (n_assign + n_experts * (tm - 1)) // tm
    src = jnp.zeros((n_tiles * tm,), jnp.int32).at[pos].set(jnp.arange(n_assign, dtype=jnp.int32) // TOP_K)
    tile_start = jnp.arange(n_tiles, dtype=jnp.int32) * tm
    tile_expert = jnp.sum((tile_start[:, None] >= ends[None, :]).astype(jnp.int32), axis=1)
    tile_expert = jnp.minimum(tile_expert, n_experts - 1)
    tile_valid = (tile_start < ends[-1]).astype(jnp.int32)
    return pos, src, tile_expert, tile_valid


def hier_moe(x1, route, gf, w_gate, w_up, w_down, layer, gn, final_norm, split, tm=256):
    n_experts = w_gate.shape[1]
    pos, src, tile_expert, tile_valid = _dispatch_plan(route, n_experts, tm)
    y = grouped_mlp(x1, src, gf, w_gate, w_up, w_down, layer, tile_expert, tile_valid, tm)
    return combine(x1, route, y, pos, gn, final_norm, split)


def _rope_tables(pos):
    half = ROPE_DIM // 2
    inv_freq = ROPE_THETA ** (-jnp.arange(half, dtype=F32) / half)
    ang = pos.astype(F32)[:, None] * inv_freq[None, :]
    z = jnp.zeros((pos.shape[0], LANES - ROPE_DIM), F32)
    cos, sin = jnp.cos(ang), jnp.sin(ang)
    return jnp.concatenate([cos, cos, z], axis=1), jnp.concatenate([sin, sin, z], axis=1)


def _swap_halves(w):
    half = ROPE_DIM // 2
    return jnp.concatenate([-w[..., half:], w[..., :half]], axis=-1)


def kernel(x_prompt, x_sample, mem_prompt, cache_mla_ckv, cache_mla_krope, cache_mem_k, cache_mem_v, state_rglru_h, state_rglru_conv, norm_mix, w_in, conv_w, conv_b, lru_wa, lru_ba, lru_wx, lru_bx, lru_lambda, mla_q_norm, mla_w_uq, mla_kv_norm, mla_w_uk, mla_w_uv, mem_norm, mem_w_k, mem_w_v, w_branch_rec, w_branch_mla, w_branch_mem, w_out, norm_ffn, router_group_w, router_group_b, router_expert_w, router_expert_b, moe_w_gate, moe_w_up, moe_w_down, norm_final):
    bp, sp, d = x_prompt.shape
    bs, ss, _ = x_sample.shape
    depth = w_in.shape[0]
    past = cache_mla_ckv.shape[2]
    m_tok = mem_prompt.shape[1]
    d_rnn = conv_w.shape[2]
    q_lora = mla_q_norm.shape[1]
    kv_lora = mla_kv_norm.shape[1]
    n_heads = mla_w_uq.shape[2]
    mem_heads = cache_mem_k.shape[3]
    mem_hd = cache_mem_k.shape[4]
    mem_q = mem_heads * mem_hd
    n_experts = moe_w_gate.shape[1]
    tp, tsm = bp * sp, bs * ss
    scale = (NOPE_DIM + ROPE_DIM) ** -0.5 * LOG2_E

    x = jnp.concatenate([x_prompt.reshape(tp, d), x_sample.reshape(tsm, d)], axis=0)
    pos_all = jnp.concatenate([jnp.tile(jnp.arange(sp), bp), jnp.tile(past + jnp.arange(ss), bs)])
    cos, sin = _rope_tables(pos_all)
    mem_flat = mem_prompt.reshape(bp * m_tok, d)
    conv_zero = jnp.zeros((bp, CONV_W - 1, d_rnn), F32)
    h_zero = jnp.zeros((bp, 1, d_rnn), F32)
    row2 = lambda v: v.reshape(1, -1)

    outs = {k: [] for k in ("p_ckv", "p_kr", "p_mk", "p_mv", "p_h", "p_conv", "s_ckv", "s_kr", "s_h", "s_conv")}
    for l in range(depth):
        o = 0
        segs = []
        for wdt in (d_rnn, d_rnn, q_lora, kv_lora, ROPE_DIM, mem_q, 3 * d):
            segs.append(w_in[l][:, o:o + wdt])
            o += wdt
        w_ux, w_uy, w_cq, w_ckv, w_kr, w_qm, w_g = segs
        w_xym = jnp.concatenate([w_ux, w_uy, w_qm], axis=1).astype(BF16)
        w_cq = w_cq.astype(BF16)
        w_kv = jnp.concatenate([w_ckv, w_kr, _swap_halves(w_kr)], axis=1).astype(BF16)
        w_g = w_g.astype(BF16)
        uq = mla_w_uq[l]
        uq_rope = uq[..., NOPE_DIM:]
        w_uq = jnp.concatenate([uq, _swap_halves(uq_rope)], axis=-1).reshape(q_lora, n_heads * HEAD_PAD).astype(BF16)
        w_uk = mla_w_uk[l].reshape(kv_lora, n_heads * NOPE_DIM).astype(BF16)
        w_uv = mla_w_uv[l].reshape(kv_lora, n_heads * V_DIM).astype(BF16)
        w_ukt = jnp.transpose(mla_w_uk[l], (1, 2, 0)).astype(BF16)
        w_uvh = jnp.transpose(mla_w_uv[l], (1, 0, 2)).astype(BF16)
        w_memkv = jnp.concatenate([mem_w_k[l], mem_w_v[l]], axis=1).astype(BF16)
        w_r32 = jnp.concatenate([router_group_w[l], router_expert_w[l],
                                 jnp.zeros((d, LANES - N_GROUPS - n_experts), F32)], axis=1)
        w_r = w_r32.astype(BF16)
        w_r_lo = (w_r32 - w_r.astype(F32)).astype(BF16)
        b_r = jnp.concatenate([router_group_b[l], router_expert_b[l],
                               jnp.zeros((LANES - N_GROUPS - n_experts,), F32)]).reshape(1, LANES)

        mkv, _ = norm_matmul(mem_flat, row2(mem_norm[l]), w_memkv, F32)
        mk = mkv[:, :mem_q].reshape(bp, m_tok, mem_heads, -1)
        mv = mkv[:, mem_q:].reshape(bp, m_tok, mem_heads, -1)
        outs["p_mk"].append(mk)
        outs["p_mv"].append(mv)

        z, xn = norm_matmul(x, row2(norm_mix[l]), w_xym, BF16)

        lru = (conv_w[l], row2(conv_b[l]), lru_wa[l].astype(BF16), lru_wx[l].astype(BF16),
               row2(lru_ba[l]), row2(lru_bx[l]), row2(lru_lambda[l]))
        rec_p, hl_p, cb_p = rglru(z, 0, bp, sp, conv_zero, h_zero, *lru, d_rnn)
        rec_s, hl_s, cb_s = rglru(z, tp, bs, ss, state_rglru_conv[l], state_rglru_h[l].reshape(bs, 1, d_rnn),
                                  *lru, d_rnn)
        outs["p_h"].append(hl_p.reshape(bp, d_rnn)); outs["p_conv"].append(cb_p)
        outs["s_h"].append(hl_s.reshape(bs, d_rnn)); outs["s_conv"].append(cb_s)

        q = q_proj(xn, w_cq, row2(mla_q_norm[l]), w_uq, cos, sin, n_heads, scale)
        ckv, kr, k, v = kv_proj(xn, w_kv, row2(mla_kv_norm[l]), w_uk, w_uv, cos, sin, n_heads)
        outs["p_ckv"].append(ckv[:tp].reshape(bp, sp, kv_lora)); outs["p_kr"].append(kr[:tp].reshape(bp, sp, ROPE_DIM))
        outs["s_ckv"].append(ckv[tp:].reshape(bs, ss, kv_lora)); outs["s_kr"].append(kr[tp:].reshape(bs, ss, ROPE_DIM))
        mla_p = attn_prompt(q, k, v, bp, sp)
        mla_s = attn_cached(q, ckv, kr, cache_mla_ckv, cache_mla_krope, l, w_ukt, w_uvh, tp, bs, ss)

        col_blk = (2 * d_rnn) // mem_q
        mkv4 = mkv.reshape(1, bp, m_tok, 2 * mem_q)
        mem_p = mem_attn(z, col_blk, mkv4, mkv4, 0, 0, bp, sp, mem_heads, mem_hd, v_head0=mem_heads)
        mem_s = mem_attn_cached(z, col_blk, cache_mem_k, cache_mem_v, l, tp, bs, ss)

        merged = merge_branches(xn, (rec_p, mla_p, mem_p), (rec_s, mla_s, mem_s), w_g,
                                w_branch_rec[l].astype(BF16),
                                w_branch_mla[l].astype(BF16), w_branch_mem[l].astype(BF16))
        x1, route = out_router(x, merged, w_out[l].astype(BF16), row2(norm_ffn[l]), w_r, w_r_lo, b_r, n_experts)
        x = hier_moe(x1, route, row2(norm_ffn[l]), moe_w_gate, moe_w_up, moe_w_down, l,
                     row2(norm_final), l == depth - 1, (tp, tsm) if l == depth - 1 else None)

    st = lambda name: jnp.stack(outs[name])
    y_prompt, y_sample = x
    return (y_prompt.reshape(bp, sp, d), y_sample.reshape(bs, ss, d),
            st("p_ckv"), st("p_kr"), st("p_mk"), st("p_mv"), st("p_h"), st("p_conv"),
            st("s_ckv"), st("s_kr"), st("s_h"), st("s_conv"))
```

```python
import functools

import jax
import jax.numpy as jnp
from jax import lax
from jax.experimental import pallas as pl
from jax.experimental.pallas import tpu as pltpu

F32 = jnp.float32
BF16 = jnp.bfloat16

EPS = 1e-6
CHUNK = 64
LRU_C = 8.0
ROPE_THETA = 10000.0
CONV_W = 4
NOPE_DIM = 128
ROPE_DIM = 64
V_DIM = 128
N_GROUPS = 4
TOP_K = 2

LANES = 128
SUBLANES = 8
HEAD_PAD = 2 * LANES
V7X_VMEM_BYTES = 64 * 1024 * 1024
VMEM_LIMIT = (V7X_VMEM_BYTES * 7) // 8
NEG_BIG = -1e30
LOG2_E = 1.4426950408889634


def _params(*sem):
    return pltpu.CompilerParams(dimension_semantics=sem, vmem_limit_bytes=VMEM_LIMIT)


def _tile(n, pref, mult=SUBLANES):
    t = min(pref, n)
    t -= t % mult
    while n % t:
        t -= mult
    return t


def _rms(x, g):
    y = x * lax.rsqrt(jnp.mean(x * x, axis=-1, keepdims=True) + EPS)
    return y * g


def _gelu(x):
    return 0.5 * x * (1.0 + jnp.tanh(0.7978845608028654 * (x + 0.044715 * (x * x * x))))


def _sigmoid(x):
    return 1.0 / (1.0 + jnp.exp(-x))


def _dot(a, b):
    return jnp.dot(a, b, preferred_element_type=F32)


def _dot_t(a, b):
    return lax.dot_general(a, b, (((1,), (1,)), ((), ())), preferred_element_type=F32)


def _norm_matmul_body(x_ref, g_ref, w_ref, z_ref, xn_ref, xn_scr):
    @pl.when(pl.program_id(1) == 0)
    def _():
        xn = _rms(x_ref[...], g_ref[...]).astype(BF16)
        xn_scr[...] = xn
        xn_ref[...] = xn

    z_ref[...] = _dot(xn_scr[...], w_ref[...]).astype(z_ref.dtype)


def norm_matmul(x, g, w, out_dtype, tm_pref=768, tn_pref=1024):
    t, d = x.shape
    n = w.shape[1]
    tm, tn = _tile(t, tm_pref, 16), _tile(n, tn_pref, LANES)
    return pl.pallas_call(
        _norm_matmul_body,
        grid=(t // tm, n // tn),
        in_specs=[
            pl.BlockSpec((tm, d), lambda i, j: (i, 0)),
            pl.BlockSpec((1, d), lambda i, j: (0, 0)),
            pl.BlockSpec((d, tn), lambda i, j: (0, j)),
        ],
        out_specs=[
            pl.BlockSpec((tm, tn), lambda i, j: (i, j)),
            pl.BlockSpec((tm, d), lambda i, j: (i, 0)),
        ],
        out_shape=[jax.ShapeDtypeStruct((t, n), out_dtype), jax.ShapeDtypeStruct((t, d), BF16)],
        scratch_shapes=[pltpu.VMEM((tm, d), BF16)],
        compiler_params=_params("arbitrary", "arbitrary"),
        name="norm_matmul",
    )(x, g, w)


def _rope_mix(r, c, s):
    return r * c + pltpu.roll(r, ROPE_DIM, 1) * s


def _q_proj_body(xn_ref, wq_ref, gq_ref, wuq_ref, cos_ref, sin_ref, q_ref, *, n_heads, scale):
    cq = _dot(xn_ref[...], wq_ref[...])
    cqn = _rms(cq, gq_ref[...]).astype(BF16)
    c = cos_ref[...] * scale
    s = sin_ref[...] * scale
    for h in range(n_heads):
        qh = _dot(cqn, wuq_ref[:, h * HEAD_PAD:(h + 1) * HEAD_PAD])
        q_ref[h, :, :NOPE_DIM] = (qh[:, :NOPE_DIM] * scale).astype(BF16)
        q_ref[h, :, NOPE_DIM:] = _rope_mix(qh[:, NOPE_DIM:], c, s).astype(BF16)


def q_proj(xn, wq, gq, wuq, cos, sin, n_heads, scale, tm_pref=512):
    t, d = xn.shape
    ql = wq.shape[1]
    tm = _tile(t, tm_pref, 16)
    return pl.pallas_call(
        functools.partial(_q_proj_body, n_heads=n_heads, scale=scale),
        grid=(t // tm,),
        in_specs=[
            pl.BlockSpec((tm, d), lambda i: (i, 0)),
            pl.BlockSpec((d, ql), lambda i: (0, 0)),
            pl.BlockSpec((1, ql), lambda i: (0, 0)),
            pl.BlockSpec((ql, n_heads * HEAD_PAD), lambda i: (0, 0)),
            pl.BlockSpec((tm, LANES), lambda i: (i, 0)),
            pl.BlockSpec((tm, LANES), lambda i: (i, 0)),
        ],
        out_specs=pl.BlockSpec((n_heads, tm, HEAD_PAD), lambda i: (0, i, 0)),
        out_shape=jax.ShapeDtypeStruct((n_heads, t, HEAD_PAD), BF16),
        compiler_params=_params("arbitrary"),
        name="q_proj",
    )(xn, wq, gq, wuq, cos, sin)


def _kv_proj_body(xn_ref, wkv_ref, gkv_ref, wuk_ref, wuv_ref, cos_ref, sin_ref,
                  ckv_ref, kr_ref, k_ref, v_ref, *, n_heads, kv_lora):
    ckr = _dot(xn_ref[...], wkv_ref[...])
    ckv = _rms(ckr[:, :kv_lora], gkv_ref[...])
    ckv_ref[...] = ckv
    rr = _rope_mix(ckr[:, kv_lora:], cos_ref[...], sin_ref[...])
    kr_ref[...] = rr[:, :ROPE_DIM]
    ckv_b = ckv.astype(BF16)
    rr_b = rr.astype(BF16)
    kn = _dot(ckv_b, wuk_ref[...])
    vv = _dot(ckv_b, wuv_ref[...])
    for h in range(n_heads):
        k_ref[h, :, :NOPE_DIM] = kn[:, h * NOPE_DIM:(h + 1) * NOPE_DIM].astype(BF16)
        k_ref[h, :, NOPE_DIM:] = rr_b
        v_ref[h] = vv[:, h * V_DIM:(h + 1) * V_DIM].astype(BF16)


def kv_proj(xn, wkv, gkv, wuk, wuv, cos, sin, n_heads, tm_pref=512):
    t, d = xn.shape
    kv_lora = gkv.shape[1]
    tm = _tile(t, tm_pref, 16)
    return pl.pallas_call(
        functools.partial(_kv_proj_body, n_heads=n_heads, kv_lora=kv_lora),
        grid=(t // tm,),
        in_specs=[
            pl.BlockSpec((tm, d), lambda i: (i, 0)),
            pl.BlockSpec((d, kv_lora + LANES), lambda i: (0, 0)),
            pl.BlockSpec((1, kv_lora), lambda i: (0, 0)),
            pl.BlockSpec((kv_lora, n_heads * NOPE_DIM), lambda i: (0, 0)),
            pl.BlockSpec((kv_lora, n_heads * V_DIM), lambda i: (0, 0)),
            pl.BlockSpec((tm, LANES), lambda i: (i, 0)),
            pl.BlockSpec((tm, LANES), lambda i: (i, 0)),
        ],
        out_specs=[
            pl.BlockSpec((tm, kv_lora), lambda i: (i, 0)),
            pl.BlockSpec((tm, ROPE_DIM), lambda i: (i, 0)),
            pl.BlockSpec((n_heads, tm, HEAD_PAD), lambda i: (0, i, 0)),
            pl.BlockSpec((n_heads, tm, V_DIM), lambda i: (0, i, 0)),
        ],
        out_shape=[
            jax.ShapeDtypeStruct((t, kv_lora), F32),
            jax.ShapeDtypeStruct((t, ROPE_DIM), F32),
            jax.ShapeDtypeStruct((n_heads, t, HEAD_PAD), BF16),
            jax.ShapeDtypeStruct((n_heads, t, V_DIM), BF16),
        ],
        compiler_params=_params("arbitrary"),
        name="kv_proj",
    )(xn, wkv, gkv, wuk, wuv, cos, sin)


def _rglru_body(ux_ref, uy_ref, conv0_ref, h0_ref, cw_ref, cb_ref, wa_ref, wx_ref, ba_ref, bx_ref,
                lam_ref, o_ref, hlast_ref, nconv_ref, ubuf, hcar, *, ts, n_blocks):
    hist = SUBLANES
    col = lambda c: slice(c * LANES, (c + 1) * LANES)
    row_id = lax.broadcasted_iota(jnp.int32, (ts, LANES), 0)

    @pl.when(pl.program_id(1) == 0)
    def _():
        ubuf[hist - (CONV_W - 1):hist, :] = conv0_ref[0]
        hcar[...] = h0_ref[0]

    ubuf[hist:hist + ts, :] = ux_ref[...].astype(F32)
    cw = cw_ref[...]
    xc = cb_ref[...]
    for k in range(CONV_W):
        off = hist - (CONV_W - 1) + k
        xc = xc + ubuf[off:off + ts, :] * cw[k:k + 1, :]
    tail = ubuf[hist + ts - (CONV_W - 1):hist + ts, :]
    nconv_ref[0] = tail
    ubuf[hist - (CONV_W - 1):hist, :] = tail

    lam = lam_ref[...]
    sp = jnp.maximum(-lam, 0.0) + jnp.log(1.0 + jnp.exp(-jnp.abs(lam)))
    for n in range(n_blocks):
        sl = col(n)
        xcn = xc[:, sl]
        xb = xcn.astype(BF16)
        r = _sigmoid(_dot(xb, wa_ref[n]) + ba_ref[:, sl])
        i = _sigmoid(_dot(xb, wx_ref[n]) + bx_ref[:, sl])
        a = jnp.exp((-LRU_C) * r * sp[:, sl])
        b = jnp.sqrt(1.0 - a * a) * (i * xcn)
        shift = 1
        while shift < ts:
            if shift < SUBLANES:
                keep = row_id >= shift
                b = b + a * jnp.where(keep, pltpu.roll(b, shift, 0), 0.0)
                a = a * jnp.where(keep, pltpu.roll(a, shift, 0), 1.0)
            else:
                b = jnp.concatenate([b[:shift], b[shift:] + a[shift:] * b[:-shift]], axis=0)
                a = jnp.concatenate([a[:shift], a[shift:] * a[:-shift]], axis=0)
            shift *= 2
        hfin = b + a * hcar[:, sl]
        hcar[:, sl] = hfin[ts - 1:ts, :]
        o_ref[:, sl] = (hfin * _gelu(uy_ref[:, sl].astype(F32))).astype(o_ref.dtype)
    hlast_ref[0] = hcar[...]


def rglru(z, row0, n_batch, seq, conv0, h0, cw, cb, wa, wx, ba, bx, lam, d_rnn, ts_pref=256):
    ts = _tile(seq, ts_pref, 16)
    nst = seq // ts
    blk0 = row0 // ts
    n_blocks = wa.shape[0]
    assert d_rnn == n_blocks * LANES and ts % SUBLANES == 0
    vec = pl.BlockSpec((1, d_rnn), lambda b, s: (0, 0))
    in_specs = [
        pl.BlockSpec((ts, d_rnn), lambda b, s: (blk0 + b * nst + s, 0)),
        pl.BlockSpec((ts, d_rnn), lambda b, s: (blk0 + b * nst + s, 1)),
        pl.BlockSpec((1, CONV_W - 1, d_rnn), lambda b, s: (b, 0, 0)),
        pl.BlockSpec((1, 1, d_rnn), lambda b, s: (b, 0, 0)),
        pl.BlockSpec((CONV_W, d_rnn), lambda b, s: (0, 0)),
        vec,
        pl.BlockSpec(wa.shape, lambda b, s: (0, 0, 0)),
        pl.BlockSpec(wx.shape, lambda b, s: (0, 0, 0)),
        vec, vec, vec,
    ]
    return pl.pallas_call(
        functools.partial(_rglru_body, ts=ts, n_blocks=n_blocks),
        grid=(n_batch, nst),
        in_specs=in_specs,
        out_specs=[
            pl.BlockSpec((ts, d_rnn), lambda b, s: (b * nst + s, 0)),
            pl.BlockSpec((1, 1, d_rnn), lambda b, s: (b, 0, 0)),
            pl.BlockSpec((1, CONV_W - 1, d_rnn), lambda b, s: (b, 0, 0)),
        ],
        out_shape=[
            jax.ShapeDtypeStruct((n_batch * seq, d_rnn), BF16),
            jax.ShapeDtypeStruct((n_batch, 1, d_rnn), F32),
            jax.ShapeDtypeStruct((n_batch, CONV_W - 1, d_rnn), F32),
        ],
        scratch_shapes=[
            pltpu.VMEM((SUBLANES + ts, d_rnn), F32),
            pltpu.VMEM((1, d_rnn), F32),
        ],
        compiler_params=_params("arbitrary", "arbitrary"),
        name="rglru",
    )(z, z, conv0, h0, cw, cb, wa, wx, ba, bx, lam)


def _softmax_step(s, v, m, l, acc):
    m_new = jnp.maximum(m, jnp.max(s, axis=-1, keepdims=True))
    alpha = jnp.exp2(m - m_new)
    p = jnp.exp2(s - m_new)
    l = alpha * l + jnp.sum(p, axis=-1, keepdims=True)
    acc = alpha * acc + _dot(p.astype(BF16), v)
    return m_new, l, acc


def _attn_prompt_body(q_ref, k_ref, v_ref, o_ref, *, tq, tk, hpb):
    qi = pl.program_id(2)
    qs = [q_ref[h] for h in range(hpb)]

    def kv(h, j):
        off = pl.multiple_of(j * tk, tk)
        return k_ref[h, pl.ds(off, tk), :], v_ref[h, pl.ds(off, tk), :]

    def body(j, carry):
        out = []
        for h in range(hpb):
            k, v = kv(h, j)
            out.append(_softmax_step(_dot_t(qs[h], k), v, *carry[h]))
        return tuple(out)

    init = (jnp.full((tq, 1), NEG_BIG, F32), jnp.zeros((tq, 1), F32), jnp.zeros((tq, V_DIM), F32))
    n_full = (qi * tq) // tk
    carry = lax.fori_loop(0, n_full, body, (init,) * hpb)
    shift = CHUNK.bit_length() - 1
    off = pl.multiple_of(n_full * tk, tq)
    q_off = qi * tq - off

    def boundary(width):
        row_chunk = (lax.broadcasted_iota(jnp.int32, (tq, width), 0) + q_off) >> shift
        col_chunk = lax.broadcasted_iota(jnp.int32, (tq, width), 1) >> shift
        visible = col_chunk <= row_chunk
        for h in range(hpb):
            k = k_ref[h, pl.ds(off, width), :]
            v = v_ref[h, pl.ds(off, width), :]
            s = jnp.where(visible, _dot_t(qs[h], k), NEG_BIG)
            m, l, acc = _softmax_step(s, v, *carry[h])
            o_ref[:, h * V_DIM:(h + 1) * V_DIM] = (acc / l).astype(o_ref.dtype)

    if tk == tq:
        boundary(tq)
    else:
        @pl.when(q_off == 0)
        def _():
            boundary(tq)

        @pl.when(q_off != 0)
        def _():
            boundary(tk)


def attn_prompt(q, k, v, n_batch, seq, tq_pref=512, tk_pref=1024, hpb=2):
    n_heads = q.shape[0]
    assert n_heads % hpb == 0
    tq = _tile(seq, tq_pref, CHUNK)
    tk = _tile(seq, tk_pref, tq)
    nq = seq // tq
    return pl.pallas_call(
        functools.partial(_attn_prompt_body, tq=tq, tk=tk, hpb=hpb),
        grid=(n_batch, n_heads // hpb, nq),
        in_specs=[
            pl.BlockSpec((hpb, tq, HEAD_PAD), lambda b, h, i: (h, b * nq + i, 0)),
            pl.BlockSpec((hpb, seq, HEAD_PAD), lambda b, h, i: (h, b, 0)),
            pl.BlockSpec((hpb, seq, V_DIM), lambda b, h, i: (h, b, 0)),
        ],
        out_specs=pl.BlockSpec((tq, hpb * V_DIM), lambda b, h, i: (b * nq + i, h)),
        out_shape=jax.ShapeDtypeStruct((n_batch * seq, n_heads * V_DIM), BF16),
        compiler_params=_params("arbitrary", "arbitrary", "arbitrary"),
        name="attn_prompt",
    )(q, k, v)


def _attn_cached_body(q_ref, ckvn_ref, krn_ref, cckv_ref, ckr_ref, wukt_ref, wuv_ref, o_ref,
                      ql_scr, qr_scr, *, n_heads, sq, tk, past):
    for h in range(n_heads):
        qh = q_ref[h]
        ql_scr[h * sq:(h + 1) * sq, :] = _dot(qh[:, :NOPE_DIM], wukt_ref[h]).astype(BF16)
        qr_scr[h * sq:(h + 1) * sq, :] = qh[:, NOPE_DIM:NOPE_DIM + ROPE_DIM]
    ql = ql_scr[...]
    qr = qr_scr[...]
    rows = n_heads * sq
    kv_lora = ql.shape[1]

    def body(j, carry):
        off = pl.multiple_of(j * tk, tk)
        kc = cckv_ref[0, 0, pl.ds(off, tk), :].astype(BF16)
        kr = ckr_ref[0, 0, pl.ds(off, tk), :].astype(BF16)
        return _softmax_step(_dot_t(ql, kc) + _dot_t(qr, kr), kc, *carry)

    init = (jnp.full((rows, 1), NEG_BIG, F32), jnp.zeros((rows, 1), F32), jnp.zeros((rows, kv_lora), F32))
    carry = lax.fori_loop(0, past // tk, body, init)
    kc = ckvn_ref[...].astype(BF16)
    kr = krn_ref[...].astype(BF16)
    m, l, acc = _softmax_step(_dot_t(ql, kc) + _dot_t(qr, kr), kc, *carry)
    o_lat = (acc / l).astype(BF16)
    for h in range(n_heads):
        o_ref[:, h * V_DIM:(h + 1) * V_DIM] = _dot(o_lat[h * sq:(h + 1) * sq, :], wuv_ref[h]).astype(o_ref.dtype)


def attn_cached(q, ckv, kr, cache_ckv, cache_kr, layer, wukt, wuv, row0, n_batch, sq, tk_pref=1024):
    n_heads = q.shape[0]
    kv_lora = ckv.shape[1]
    past = cache_ckv.shape[2]
    tk = _tile(past, tk_pref, 16)
    blk0 = row0 // sq
    return pl.pallas_call(
        functools.partial(_attn_cached_body, n_heads=n_heads, sq=sq, tk=tk, past=past),
        grid=(n_batch,),
        in_specs=[
            pl.BlockSpec((n_heads, sq, HEAD_PAD), lambda b: (0, blk0 + b, 0)),
            pl.BlockSpec((sq, kv_lora), lambda b: (blk0 + b, 0)),
            pl.BlockSpec((sq, ROPE_DIM), lambda b: (blk0 + b, 0)),
            pl.BlockSpec((1, 1, past, kv_lora), lambda b: (layer, b, 0, 0)),
            pl.BlockSpec((1, 1, past, ROPE_DIM), lambda b: (layer, b, 0, 0)),
            pl.BlockSpec(wukt.shape, lambda b: (0, 0, 0)),
            pl.BlockSpec(wuv.shape, lambda b: (0, 0, 0)),
        ],
        out_specs=pl.BlockSpec((sq, n_heads * V_DIM), lambda b: (b, 0)),
        out_shape=jax.ShapeDtypeStruct((n_batch * sq, n_heads * V_DIM), BF16),
        scratch_shapes=[pltpu.VMEM((n_heads * sq, kv_lora), BF16), pltpu.VMEM((n_heads * sq, ROPE_DIM), BF16)],
        compiler_params=_params("arbitrary"),
        name="attn_cached",
    )(q, ckv, kr, cache_ckv, cache_kr, wukt, wuv)


def _mem_attn_body(q_ref, *refs, n_heads):
    k_refs, v_refs, o_ref = refs[:n_heads], refs[n_heads:2 * n_heads], refs[2 * n_heads]
    hd = q_ref.shape[1] // n_heads
    scale = hd ** -0.5
    for h in range(n_heads):
        sl = slice(h * hd, (h + 1) * hd)
        s = _dot_t(q_ref[:, sl], k_refs[h][...].astype(BF16)) * scale
        e = jnp.exp(s - jnp.max(s, axis=-1, keepdims=True))
        o = _dot(e.astype(BF16), v_refs[h][...].astype(BF16)) / jnp.sum(e, axis=-1, keepdims=True)
        o_ref[:, sl] = o.astype(o_ref.dtype)


def mem_attn(z, col_blk, mem_k, mem_v, layer, row0, n_batch, seq, n_heads, hd, v_head0=0, tm_pref=512):
    m_tok = mem_k.shape[2]
    dq = n_heads * hd
    tm = _tile(seq, tm_pref, 16)
    nst = seq // tm
    blk0 = row0 // tm

    def head_spec(h):
        return pl.BlockSpec((None, None, m_tok, hd), lambda b, s: (layer, b, 0, h))

    k_specs = [head_spec(h) for h in range(n_heads)]
    v_specs = [head_spec(v_head0 + h) for h in range(n_heads)]
    return pl.pallas_call(
        functools.partial(_mem_attn_body, n_heads=n_heads),
        grid=(n_batch, nst),
        in_specs=[pl.BlockSpec((tm, dq), lambda b, s: (blk0 + b * nst + s, col_blk))] + k_specs + v_specs,
        out_specs=pl.BlockSpec((tm, dq), lambda b, s: (b * nst + s, 0)),
        out_shape=jax.ShapeDtypeStruct((n_batch * seq, dq), BF16),
        compiler_params=_params("arbitrary", "arbitrary"),
        name="mem_attn",
    )(z, *([mem_k] * n_heads), *([mem_v] * n_heads))


def _mem_attn_cached_body(q_ref, k_hbm, v_hbm, o_ref, kbuf, vbuf, sem, *, layer, n_heads):
    b = pl.program_id(0)
    n = pl.num_programs(0)

    def copies(batch, slot):
        out = []
        for h in range(n_heads):
            out.append(pltpu.make_async_copy(k_hbm.at[layer, batch, :, h, :], kbuf.at[slot, h], sem.at[slot]))
            out.append(pltpu.make_async_copy(v_hbm.at[layer, batch, :, h, :], vbuf.at[slot, h], sem.at[slot]))
        return out

    @pl.when(b == 0)
    def _():
        for c in copies(0, 0):
            c.start()

    @pl.when(b + 1 < n)
    def _():
        for c in copies(b + 1, (b + 1) % 2):
            c.start()

    slot = b % 2
    for c in copies(b, slot):
        c.wait()
    hd = kbuf.shape[3]
    scale = hd ** -0.5
    for h in range(n_heads):
        sl = slice(h * hd, (h + 1) * hd)
        s = _dot_t(q_ref[:, sl], kbuf[slot, h].astype(BF16)) * scale
        e = jnp.exp(s - jnp.max(s, axis=-1, keepdims=True))
        o = _dot(e.astype(BF16), vbuf[slot, h].astype(BF16)) / jnp.sum(e, axis=-1, keepdims=True)
        o_ref[:, sl] = o.astype(o_ref.dtype)


def mem_attn_cached(z, col_blk, mem_k, mem_v, layer, row0, n_batch, seq):
    _, _, m_tok, n_heads, hd = mem_k.shape
    dq = n_heads * hd
    blk0 = row0 // seq
    return pl.pallas_call(
        functools.partial(_mem_attn_cached_body, layer=layer, n_heads=n_heads),
        grid=(n_batch,),
        in_specs=[
            pl.BlockSpec((seq, dq), lambda b: (blk0 + b, col_blk)),
            pl.BlockSpec(memory_space=pl.ANY),
            pl.BlockSpec(memory_space=pl.ANY),
        ],
        out_specs=pl.BlockSpec((seq, dq), lambda b: (b, 0)),
        out_shape=jax.ShapeDtypeStruct((n_batch * seq, dq), BF16),
        scratch_shapes=[pltpu.VMEM((2, n_heads, m_tok, hd), F32), pltpu.VMEM((2, n_heads, m_tok, hd), F32),
                        pltpu.SemaphoreType.DMA((2,))],
        compiler_params=_params("arbitrary"),
        name="mem_attn_cached",
    )(z, mem_k, mem_v)


def _merge_body(xn_ref, p0_ref, p1_ref, p2_ref, s0_ref, s1_ref, s2_ref,
                g0_ref, g1_ref, g2_ref, w0_ref, w1_ref, w2_ref, o_ref, *, n_first):
    def emit(a0_ref, a1_ref, a2_ref):
        xn = xn_ref[...]
        acc = _sigmoid(_dot(xn, g0_ref[...])) * _dot(a0_ref[...], w0_ref[...])
        acc = acc + _sigmoid(_dot(xn, g1_ref[...])) * _dot(a1_ref[...], w1_ref[...])
        acc = acc + _sigmoid(_dot(xn, g2_ref[...])) * _dot(a2_ref[...], w2_ref[...])
        o_ref[...] = acc.astype(o_ref.dtype)

    i = pl.program_id(1)

    @pl.when(i < n_first)
    def _():
        emit(p0_ref, p1_ref, p2_ref)

    @pl.when(i >= n_first)
    def _():
        emit(s0_ref, s1_ref, s2_ref)


def merge_branches(xn, first, second, wg, w0, w1, w2, tm_pref=256, tn_pref=512):
    t, d = xn.shape
    t1, t2 = first[0].shape[0], second[0].shape[0]
    tm, tn = _tile(min(t1, t2), tm_pref, 16), _tile(d, tn_pref, LANES)
    assert t1 % tm == 0 and t2 % tm == 0 and t1 + t2 == t
    n1 = t1 // tm
    nj = d // tn
    act = pl.BlockSpec((tm, d), lambda j, i: (i, 0))
    act1 = pl.BlockSpec((tm, d), lambda j, i: (jnp.minimum(i, n1 - 1), 0))
    act2 = pl.BlockSpec((tm, d), lambda j, i: (jnp.maximum(i - n1, 0), 0))
    wsp = pl.BlockSpec((d, tn), lambda j, i: (0, j))
    return pl.pallas_call(
        functools.partial(_merge_body, n_first=n1),
        grid=(nj, t // tm),
        in_specs=[act, act1, act1, act1, act2, act2, act2,
                  pl.BlockSpec((d, tn), lambda j, i: (0, j)),
                  pl.BlockSpec((d, tn), lambda j, i: (0, nj + j)),
                  pl.BlockSpec((d, tn), lambda j, i: (0, 2 * nj + j)),
                  wsp, wsp, wsp],
        out_specs=pl.BlockSpec((tm, tn), lambda j, i: (i, j)),
        out_shape=jax.ShapeDtypeStruct((t, d), BF16),
        compiler_params=_params("arbitrary", "arbitrary"),
        name="merge_branches",
    )(xn, *first, *second, wg, wg, wg, w0, w1, w2)


def _out_router_body(x_ref, m_ref, wo_ref, gf_ref, wr_ref, wrl_ref, br_ref, x1_ref, route_ref, *, n_experts):
    x1 = x_ref[...] + _dot(m_ref[...], wo_ref[...])
    x1_ref[...] = x1
    hn = _rms(x1, gf_ref[...])
    hn_hi = hn.astype(BF16)
    hn_lo = (hn - hn_hi.astype(F32)).astype(BF16)
    lg = (_dot(hn_hi, wr_ref[...]) + (_dot(hn_lo, wr_ref[...]) + _dot(hn_hi, wrl_ref[...]))) + br_ref[...]
    lane = lax.broadcasted_iota(jnp.int32, lg.shape, 1).astype(F32)
    epg = n_experts // N_GROUPS
    is_g = lane < N_GROUPS
    gl = jnp.where(is_g, lg, -jnp.inf)
    gmax = jnp.max(gl, axis=-1, keepdims=True)
    g_sel = jnp.min(jnp.where(gl == gmax, lane, LANES), axis=-1, keepdims=True)
    pg = 1.0 / jnp.sum(jnp.where(is_g, jnp.exp(lg - gmax), 0.0), axis=-1, keepdims=True)
    lo = N_GROUPS + g_sel * epg
    el = jnp.where((lane >= lo) & (lane < lo + epg), lg, -jnp.inf)
    t1 = jnp.max(el, axis=-1, keepdims=True)
    i1 = jnp.min(jnp.where(el == t1, lane, LANES), axis=-1, keepdims=True)
    el2 = jnp.where(lane == i1, -jnp.inf, el)
    t2 = jnp.max(el2, axis=-1, keepdims=True)
    i2 = jnp.min(jnp.where(el2 == t2, lane, LANES), axis=-1, keepdims=True)
    dlt = jnp.exp(t2 - t1)
    w1 = pg / (1.0 + dlt)
    w2 = pg * dlt / (1.0 + dlt)
    e1 = i1 - N_GROUPS
    e2 = i2 - N_GROUPS
    route = jnp.where(lane == 0, e1, jnp.where(lane == 1, e2, jnp.where(lane == 2, w1, jnp.where(lane == 3, w2, 0.0))))
    route_ref[...] = route


def out_router(x, merged, wo, gf, wr, wr_lo, br, n_experts, tm_pref=512):
    t, d = x.shape
    tm = _tile(t, tm_pref, 16)
    row = pl.BlockSpec((tm, d), lambda i: (i, 0))
    return pl.pallas_call(
        functools.partial(_out_router_body, n_experts=n_experts),
        grid=(t // tm,),
        in_specs=[row, row,
                  pl.BlockSpec((d, d), lambda i: (0, 0)),
                  pl.BlockSpec((1, d), lambda i: (0, 0)),
                  pl.BlockSpec((d, LANES), lambda i: (0, 0)),
                  pl.BlockSpec((d, LANES), lambda i: (0, 0)),
                  pl.BlockSpec((1, LANES), lambda i: (0, 0))],
        out_specs=[row, pl.BlockSpec((tm, LANES), lambda i: (i, 0))],
        out_shape=[jax.ShapeDtypeStruct((t, d), F32), jax.ShapeDtypeStruct((t, LANES), F32)],
        compiler_params=_params("arbitrary"),
        name="out_router",
    )(x, merged, wo, gf, wr, wr_lo, br)


def _gmm_body(src_ref, te_ref, tv_ref, x_hbm, gf_ref, wg_ref, wu_ref, wd_ref, y_ref,
              buf, sem, wg_b, wu_b, wd_b, *, tm):
    i = pl.program_id(0)
    n = pl.num_programs(0)

    def fetch(tile, slot):
        def issue(j, carry):
            base = pl.multiple_of(j * SUBLANES, SUBLANES)
            for k in range(SUBLANES):
                row = src_ref[tile * tm + base + k]
                pltpu.make_async_copy(x_hbm.at[pl.ds(row, 1)], buf.at[slot, j, pl.ds(k, 1)],
                                      sem.at[slot]).start()
            return carry

        lax.fori_loop(0, tm // SUBLANES, issue, 0)

    @pl.when(jnp.logical_and(i == 0, tv_ref[0] > 0))
    def _():
        fetch(0, 0)

    @pl.when(jnp.logical_and(i + 1 < n, tv_ref[jnp.minimum(i + 1, n - 1)] > 0))
    def _():
        fetch(i + 1, (i + 1) % 2)

    valid = tv_ref[i] > 0
    new_expert = jnp.logical_or(i == 0, te_ref[i] != te_ref[jnp.maximum(i - 1, 0)])

    @pl.when(jnp.logical_and(valid, new_expert))
    def _():
        wg_b[...] = wg_ref[0, 0].astype(BF16)
        wu_b[...] = wu_ref[0, 0].astype(BF16)
        wd_b[...] = wd_ref[0, 0].astype(BF16)

    @pl.when(valid)
    def _():
        slot = i % 2
        for j in range(tm // SUBLANES):
            pltpu.make_async_copy(x_hbm.at[pl.ds(0, SUBLANES)], buf.at[slot, j], sem.at[slot]).wait()
        hn = _rms(buf[slot].reshape(tm, buf.shape[3]), gf_ref[...]).astype(BF16)
        a = _dot(hn, wg_b[...])
        act = (a * _sigmoid(a) * _dot(hn, wu_b[...])).astype(BF16)
        y_ref[...] = _dot(act, wd_b[...])

    @pl.when(jnp.logical_not(valid))
    def _():
        y_ref[...] = jnp.zeros_like(y_ref)


def grouped_mlp(x, src, gf, w_gate, w_up, w_down, layer, tile_expert, tile_valid, tm):
    d = x.shape[1]
    p = src.shape[0]
    d_expert = w_down.shape[2]
    w_in_spec = pl.BlockSpec((1, 1, d, d_expert), lambda i, src, te, tv: (layer, te[i], 0, 0))
    return pl.pallas_call(
        functools.partial(_gmm_body, tm=tm),
        grid_spec=pltpu.PrefetchScalarGridSpec(
            num_scalar_prefetch=3,
            grid=(p // tm,),
            in_specs=[
                pl.BlockSpec(memory_space=pl.ANY),
                pl.BlockSpec((1, d), lambda i, src, te, tv: (0, 0)),
                w_in_spec,
                w_in_spec,
                pl.BlockSpec((1, 1, d_expert, d), lambda i, src, te, tv: (layer, te[i], 0, 0)),
            ],
            out_specs=pl.BlockSpec((tm, d), lambda i, src, te, tv: (i, 0)),
            scratch_shapes=[pltpu.VMEM((2, tm // SUBLANES, SUBLANES, d), F32), pltpu.SemaphoreType.DMA((2,)),
                            pltpu.VMEM((d, d_expert), BF16), pltpu.VMEM((d, d_expert), BF16),
                            pltpu.VMEM((d_expert, d), BF16)],
        ),
        out_shape=jax.ShapeDtypeStruct((p, d), F32),
        compiler_params=_params("arbitrary"),
        name="moe_grouped_mlp",
    )(src, tile_expert, tile_valid, x, gf, w_gate, w_up, w_down)


def _combine_body(pos_ref, x1_ref, route_ref, gn_ref, y_hbm, *rest, tm, final_norm, n_first):
    outs, (buf, sem) = rest[:-2], rest[-2:]
    i = pl.program_id(0)
    n = pl.num_programs(0)

    groups = tm // SUBLANES

    def fetch(tile, slot):
        def issue(j, carry):
            for s in range(SUBLANES):
                for k in range(TOP_K):
                    src_row = pos_ref[(tile * tm + j * SUBLANES + s) * TOP_K + k]
                    pltpu.make_async_copy(y_hbm.at[pl.ds(src_row, 1)], buf.at[slot, k * groups + j, pl.ds(s, 1)],
                                          sem.at[slot]).start()
            return carry

        lax.fori_loop(0, groups, issue, 0)

    @pl.when(i == 0)
    def _():
        fetch(0, 0)

    @pl.when(i + 1 < n)
    def _():
        fetch(i + 1, (i + 1) % 2)

    slot = i % 2
    for g in range(TOP_K * groups):
        pltpu.make_async_copy(y_hbm.at[pl.ds(0, SUBLANES)], buf.at[slot, g], sem.at[slot]).wait()
    rt = route_ref[...]
    d = buf.shape[3]
    y_a = buf[slot, 0:groups].reshape(tm, d)
    y_b = buf[slot, groups:TOP_K * groups].reshape(tm, d)
    x2 = x1_ref[...] + (rt[:, 2:3] * y_a + rt[:, 3:4] * y_b)
    if final_norm:
        x2 = _rms(x2, gn_ref[...])
    if n_first is None:
        outs[0][...] = x2
    else:
        @pl.when(i < n_first)
        def _():
            outs[0][...] = x2

        @pl.when(i >= n_first)
        def _():
            outs[1][...] = x2


def combine(x1, route, y, pos, gn, final_norm, split=None, tm_pref=256):
    t, d = x1.shape
    tm = _tile(t if split is None else min(split), tm_pref, SUBLANES)
    if split is None:
        n_first = None
        out_specs = pl.BlockSpec((tm, d), lambda i, pos: (i, 0))
        out_shape = jax.ShapeDtypeStruct((t, d), F32)
    else:
        assert split[0] % tm == 0 and split[1] % tm == 0 and sum(split) == t
        n_first = split[0] // tm
        out_specs = [pl.BlockSpec((tm, d), lambda i, pos: (jnp.minimum(i, n_first - 1), 0)),
                     pl.BlockSpec((tm, d), lambda i, pos: (jnp.maximum(i - n_first, 0), 0))]
        out_shape = [jax.ShapeDtypeStruct((split[0], d), F32), jax.ShapeDtypeStruct((split[1], d), F32)]
    return pl.pallas_call(
        functools.partial(_combine_body, tm=tm, final_norm=final_norm, n_first=n_first),
        grid_spec=pltpu.PrefetchScalarGridSpec(
            num_scalar_prefetch=1,
            grid=(t // tm,),
            in_specs=[
                pl.BlockSpec((tm, d), lambda i, pos: (i, 0)),
                pl.BlockSpec((tm, LANES), lambda i, pos: (i, 0)),
                pl.BlockSpec((1, d), lambda i, pos: (0, 0)),
                pl.BlockSpec(memory_space=pl.ANY),
            ],
            out_specs=out_specs,
            scratch_shapes=[pltpu.VMEM((2, TOP_K * tm // SUBLANES, SUBLANES, d), F32),
                            pltpu.SemaphoreType.DMA((2,))],
        ),
        out_shape=out_shape,
        compiler_params=_params("arbitrary"),
        name="moe_combine",
    )(pos, x1, route, gn, y)


def _dispatch_plan(route, n_experts, tm):
    t = route.shape[0]
    n_assign = TOP_K * t
    eid = route[:, :TOP_K].astype(jnp.int32).reshape(n_assign)
    onehot = eid[:, None] == jnp.arange(n_experts, dtype=jnp.int32)[None, :]
    blk = _tile(n_assign, LANES, 1)
    n_blk = n_assign // blk
    oh = onehot.astype(BF16).reshape(n_blk, blk, n_experts)
    below = (jnp.arange(blk)[:, None] > jnp.arange(blk)[None, :]).astype(BF16)
    within = jnp.einsum("ij,bje->bie", below, oh, preferred_element_type=F32)
    blk_sum = jnp.sum(oh.astype(F32), axis=1)
    blk_off = jnp.cumsum(blk_sum, axis=0) - blk_sum
    rank_all = (within + blk_off[:, None, :]).reshape(n_assign, n_experts)
    counts = jnp.sum(blk_sum, axis=0).astype(jnp.int32)
    padded = ((counts + tm - 1) // tm) * tm
    ends = jnp.cumsum(padded)
    starts = ends - padded
    pos = jnp.sum(jnp.where(onehot, rank_all + starts.astype(F32)[None, :], 0.0), axis=1).astype(jnp.int32)
    n_tiles = (n_assign + n_experts * (tm - 1)) // tm
    src = jnp.zeros((n_tiles * tm,), jnp.int32).at[pos].set(jnp.arange(n_assign, dtype=jnp.int32) // TOP_K)
    tile_start = jnp.arange(n_tiles, dtype=jnp.int32) * tm
    tile_expert = jnp.sum((tile_start[:, None] >= ends[None, :]).astype(jnp.int32), axis=1)
    tile_expert = jnp.minimum(tile_expert, n_experts - 1)
    tile_valid = (tile_start < ends[-1]).astype(jnp.int32)
    return pos, src, tile_expert, tile_valid


def hier_moe(x1, route, gf, w_gate, w_up, w_down, layer, gn, final_norm, split, tm=512):
    n_experts = w_gate.shape[1]
    pos, src, tile_expert, tile_valid = _dispatch_plan(route, n_experts, tm)
    y = grouped_mlp(x1, src, gf, w_gate, w_up, w_down, layer, tile_expert, tile_valid, tm)
    return combine(x1, route, y, pos, gn, final_norm, split)


def _rope_tables(pos):
    half = ROPE_DIM // 2
    inv_freq = ROPE_THETA ** (-jnp.arange(half, dtype=F32) / half)
    ang = pos.astype(F32)[:, None] * inv_freq[None, :]
    z = jnp.zeros((pos.shape[0], LANES - ROPE_DIM), F32)
    cos, sin = jnp.cos(ang), jnp.sin(ang)
    return jnp.concatenate([cos, cos, z], axis=1), jnp.concatenate([sin, sin, z], axis=1)


def _swap_halves(w):
    half = ROPE_DIM // 2
    return jnp.concatenate([-w[..., half:], w[..., :half]], axis=-1)


def kernel(x_prompt, x_sample, mem_prompt, cache_mla_ckv, cache_mla_krope, cache_mem_k, cache_mem_v, state_rglru_h, state_rglru_conv, norm_mix, w_in, conv_w, conv_b, lru_wa, lru_ba, lru_wx, lru_bx, lru_lambda, mla_q_norm, mla_w_uq, mla_kv_norm, mla_w_uk, mla_w_uv, mem_norm, mem_w_k, mem_w_v, w_branch_rec, w_branch_mla, w_branch_mem, w_out, norm_ffn, router_group_w, router_group_b, router_expert_w, router_expert_b, moe_w_gate, moe_w_up, moe_w_down, norm_final):
    bp, sp, d = x_prompt.shape
    bs, ss, _ = x_sample.shape
    depth = w_in.shape[0]
    past = cache_mla_ckv.shape[2]
    m_tok = mem_prompt.shape[1]
    d_rnn = conv_w.shape[2]
    q_lora = mla_q_norm.shape[1]
    kv_lora = mla_kv_norm.shape[1]
    n_heads = mla_w_uq.shape[2]
    mem_heads = cache_mem_k.shape[3]
    mem_hd = cache_mem_k.shape[4]
    mem_q = mem_heads * mem_hd
    n_experts = moe_w_gate.shape[1]
    tp, tsm = bp * sp, bs * ss
    scale = (NOPE_DIM + ROPE_DIM) ** -0.5 * LOG2_E

    x = jnp.concatenate([x_prompt.reshape(tp, d), x_sample.reshape(tsm, d)], axis=0)
    pos_all = jnp.concatenate([jnp.tile(jnp.arange(sp), bp), jnp.tile(past + jnp.arange(ss), bs)])
    cos, sin = _rope_tables(pos_all)
    mem_flat = mem_prompt.reshape(bp * m_tok, d)
    conv_zero = jnp.zeros((bp, CONV_W - 1, d_rnn), F32)
    h_zero = jnp.zeros((bp, 1, d_rnn), F32)
    row2 = lambda v: v.reshape(1, -1)

    outs = {k: [] for k in ("p_ckv", "p_kr", "p_mk", "p_mv", "p_h", "p_conv", "s_ckv", "s_kr", "s_h", "s_conv")}
    for l in range(depth):
        o = 0
        segs = []
        for wdt in (d_rnn, d_rnn, q_lora, kv_lora, ROPE_DIM, mem_q, 3 * d):
            segs.append(w_in[l][:, o:o + wdt])
            o += wdt
        w_ux, w_uy, w_cq, w_ckv, w_kr, w_qm, w_g = segs
        w_xym = jnp.concatenate([w_ux, w_uy, w_qm], axis=1).astype(BF16)
        w_cq = w_cq.astype(BF16)
        w_kv = jnp.concatenate([w_ckv, w_kr, _swap_halves(w_kr)], axis=1).astype(BF16)
        w_g = w_g.astype(BF16)
        uq = mla_w_uq[l]
        uq_rope = uq[..., NOPE_DIM:]
        w_uq = jnp.concatenate([uq, _swap_halves(uq_rope)], axis=-1).reshape(q_lora, n_heads * HEAD_PAD).astype(BF16)
        w_uk = mla_w_uk[l].reshape(kv_lora, n_heads * NOPE_DIM).astype(BF16)
        w_uv = mla_w_uv[l].reshape(kv_lora, n_heads * V_DIM).astype(BF16)
        w_ukt = jnp.transpose(mla_w_uk[l], (1, 2, 0)).astype(BF16)
        w_uvh = jnp.transpose(mla_w_uv[l], (1, 0, 2)).astype(BF16)
        w_memkv = jnp.concatenate([mem_w_k[l], mem_w_v[l]], axis=1).astype(BF16)
        w_r32 = jnp.concatenate([router_group_w[l], router_expert_w[l],
                                 jnp.zeros((d, LANES - N_GROUPS - n_experts), F32)], axis=1)
        w_r = w_r32.astype(BF16)
        w_r_lo = (w_r32 - w_r.astype(F32)).astype(BF16)
        b_r = jnp.concatenate([router_group_b[l], router_expert_b[l],
                               jnp.zeros((LANES - N_GROUPS - n_experts,), F32)]).reshape(1, LANES)

        mkv, _ = norm_matmul(mem_flat, row2(mem_norm[l]), w_memkv, F32)
        mk = mkv[:, :mem_q].reshape(bp, m_tok, mem_heads, -1)
        mv = mkv[:, mem_q:].reshape(bp, m_tok, mem_heads, -1)
        outs["p_mk"].append(mk)
        outs["p_mv"].append(mv)

        z, xn = norm_matmul(x, row2(norm_mix[l]), w_xym, BF16)

        lru = (conv_w[l], row2(conv_b[l]), lru_wa[l].astype(BF16), lru_wx[l].astype(BF16),
               row2(lru_ba[l]), row2(lru_bx[l]), row2(lru_lambda[l]))
        rec_p, hl_p, cb_p = rglru(z, 0, bp, sp, conv_zero, h_zero, *lru, d_rnn)
        rec_s, hl_s, cb_s = rglru(z, tp, bs, ss, state_rglru_conv[l], state_rglru_h[l].reshape(bs, 1, d_rnn),
                                  *lru, d_rnn)
        outs["p_h"].append(hl_p.reshape(bp, d_rnn)); outs["p_conv"].append(cb_p)
        outs["s_h"].append(hl_s.reshape(bs, d_rnn)); outs["s_conv"].append(cb_s)

        q = q_proj(xn, w_cq, row2(mla_q_norm[l]), w_uq, cos, sin, n_heads, scale)
        ckv, kr, k, v = kv_proj(xn, w_kv, row2(mla_kv_norm[l]), w_uk, w_uv, cos, sin, n_heads)
        outs["p_ckv"].append(ckv[:tp].reshape(bp, sp, kv_lora)); outs["p_kr"].append(kr[:tp].reshape(bp, sp, ROPE_DIM))
        outs["s_ckv"].append(ckv[tp:].reshape(bs, ss, kv_lora)); outs["s_kr"].append(kr[tp:].reshape(bs, ss, ROPE_DIM))
        mla_p = attn_prompt(q, k, v, bp, sp)
        mla_s = attn_cached(q, ckv, kr, cache_mla_ckv, cache_mla_krope, l, w_ukt, w_uvh, tp, bs, ss)

        col_blk = (2 * d_rnn) // mem_q
        mkv4 = mkv.reshape(1, bp, m_tok, 2 * mem_q)
        mem_p = mem_attn(z, col_blk, mkv4, mkv4, 0, 0, bp, sp, mem_heads, mem_hd, v_head0=mem_heads)
        mem_s = mem_attn_cached(z, col_blk, cache_mem_k, cache_mem_v, l, tp, bs, ss)

        merged = merge_branches(xn, (rec_p, mla_p, mem_p), (rec_s, mla_s, mem_s), w_g,
                                w_branch_rec[l].astype(BF16),
                                w_branch_mla[l].astype(BF16), w_branch_mem[l].astype(BF16))
        x1, route = out_router(x, merged, w_out[l].astype(BF16), row2(norm_ffn[l]), w_r, w_r_lo, b_r, n_experts)
        x = hier_moe(x1, route, row2(norm_ffn[l]), moe_w_gate, moe_w_up, moe_w_down, l,
                     row2(norm_final), l == depth - 1, (tp, tsm) if l == depth - 1 else None)

    st = lambda name: jnp.stack(outs[name])
    y_prompt, y_sample = x
    return (y_prompt.reshape(bp, sp, d), y_sample.reshape(bs, ss, d),
            st("p_ckv"), st("p_kr"), st("p_mk"), st("p_mv"), st("p_h"), st("p_conv"),
            st("s_ckv"), st("s_kr"), st("s_h"), st("s_conv"))
```
